```python
import jax, jax.numpy as jnp
from jax import lax
import numpy as np

D_MODEL = 4096
BATCH = 4
SEQ = 2048
DEPTH = 1
DEC_BATCH = 128
DEC_SEQ = 8
PAST_LEN = 8192
PAGE_SIZE = 128

MIX_WIDTH = D_MODEL
ATTN_WIDTH = MIX_WIDTH // 2
POOL_WIDTH = MIX_WIDTH - ATTN_WIDTH
HEAD_DIM = 64
N_HEADS = ATTN_WIDTH // HEAD_DIM
N_KV_HEADS = N_HEADS // 8
GROUP = N_HEADS // N_KV_HEADS
KV_WIDTH = N_KV_HEADS * HEAD_DIM
WINDOW = 128
ROT_DIM = HEAD_DIM // 4
ROPE_THETA = 500000.0
POOL_WINDOWS = (2, 4, 8, 16)
N_POOL_GROUPS = len(POOL_WINDOWS)
POOL_GROUP_WIDTH = POOL_WIDTH // N_POOL_GROUPS
POOL_STATE = max(POOL_WINDOWS) - 1
IN_COLS = ATTN_WIDTH + 2 * KV_WIDTH + POOL_WIDTH
PEER_HEADS = 8
PEER_KEY_DIM = 256
PEER_HALF = PEER_KEY_DIM // 2
N_KEYS = 128
N_EXPERTS = N_KEYS * N_KEYS
PEER_TOPK = 16
PEER_BLOCK = 64
ALPHA = (2 * DEPTH) ** 0.25
BETA = (8 * DEPTH) ** -0.25
LN_EPS = 1e-5
N_MOD = 6
NEG_INF = -1e30
F32 = jnp.float32

kernel_name = 'hymba_swa_sink_pool_peer_step'


def layer_norm(x, g, b):
    xf = x.astype(F32)
    mu = xf.mean(-1, keepdims=True)
    var = jnp.square(xf - mu).mean(-1, keepdims=True)
    return ((xf - mu) * lax.rsqrt(var + LN_EPS) * g.astype(F32) + b.astype(F32)).astype(x.dtype)


def adaln_modulation(c, w_ada, b_ada):
    mod = jax.nn.silu(c) @ w_ada + b_ada
    mod = mod.reshape(c.shape[0], N_MOD, D_MODEL)[:, :, None, :]
    return tuple(mod[:, i] for i in range(N_MOD))


def rope(x, pos):
    half = ROT_DIM // 2
    inv_freq = ROPE_THETA ** (-jnp.arange(half, dtype=F32) / half)
    ang = pos.astype(F32)[:, None] * inv_freq[None, :]
    shape = (pos.shape[0],) + (1,) * (x.ndim - 3) + (half,)
    cos = jnp.cos(ang).reshape(shape).astype(x.dtype)
    sin = jnp.sin(ang).reshape(shape).astype(x.dtype)
    x1 = x[..., :half]
    x2 = x[..., half:ROT_DIM]
    return jnp.concatenate([x1 * cos - x2 * sin, x2 * cos + x1 * sin, x[..., ROT_DIM:]], axis=-1)


def sink_attend(q, k, v, mask, sinks):
    s = jnp.einsum('...qhgd,...khd->...hgqk', q, k).astype(F32) * (HEAD_DIM ** -0.5)
    s = jnp.where(mask, s, NEG_INF)
    sink = sinks.astype(F32).reshape(N_KV_HEADS, GROUP, 1, 1)
    m = jnp.maximum(s.max(-1, keepdims=True), sink)
    p = jnp.exp(s - m)
    denom = p.sum(-1, keepdims=True) + jnp.exp(sink - m)
    return jnp.einsum('...hgqk,...khd->...qhgd', (p / denom).astype(v.dtype), v)


def banded_window_attention(q, k, v, sinks):
    b, s = q.shape[0], q.shape[1]
    nb = s // WINDOW
    qb = q.reshape(b, nb, WINDOW, N_KV_HEADS, GROUP, HEAD_DIM)

    def band(t):
        tp = jnp.pad(t, ((0, 0), (WINDOW, 0), (0, 0), (0, 0))).reshape(b, nb + 1, WINDOW, N_KV_HEADS, HEAD_DIM)
        return jnp.concatenate([tp[:, :-1], tp[:, 1:]], axis=2)

    i = jnp.arange(WINDOW)[:, None]
    j = jnp.arange(2 * WINDOW)[None, :]
    blk = jnp.arange(nb)[:, None, None]
    diff = WINDOW + i - j
    mask = (diff >= 0) & (diff <= WINDOW) & (blk * WINDOW + j - WINDOW >= 0)
    o = sink_attend(qb, band(k), band(v), mask[:, None, None], sinks)
    return o.reshape(b, s, N_KV_HEADS, GROUP, HEAD_DIM)


def cached_window_attention(q, k, v, cache_k, cache_v, pos, sinks):
    wc = cache_k.shape[1]
    kc = jnp.concatenate([cache_k, k], axis=1)
    vc = jnp.concatenate([cache_v, v], axis=1)
    kpos = jnp.arange(wc + k.shape[1], dtype=jnp.int32) + (PAST_LEN - wc)
    diff = pos[:, None] - kpos[None, :]
    mask = (diff >= 0) & (diff <= WINDOW)
    o = sink_attend(q, kc, vc, mask, sinks)
    return o, kc[:, -wc:], vc[:, -wc:]


def multiscale_pool(u, prev, pos, pool_w, pool_scale):
    n, t, _ = u.shape
    ext = jnp.concatenate([prev, u], axis=1)
    cs = jnp.pad(jnp.cumsum(ext.astype(F32), axis=1), ((0, 0), (1, 0), (0, 0)))
    outs = []
    for gi, w in enumerate(POOL_WINDOWS):
        sl = slice(gi * POOL_GROUP_WIDTH, (gi + 1) * POOL_GROUP_WIDTH)
        hi = cs[:, POOL_STATE + 1:POOL_STATE + 1 + t, sl]
        lo = cs[:, POOL_STATE + 1 - w:POOL_STATE + 1 - w + t, sl]
        cnt = jnp.minimum(w, pos + 1).astype(F32)[None, :, None]
        outs.append((hi - lo) / cnt)
    pooled = jnp.stack(outs, axis=2)
    d = (pooled - u.astype(F32).reshape(n, t, N_POOL_GROUPS, POOL_GROUP_WIDTH)).astype(u.dtype)
    y = jnp.einsum('ntgc,gcd->ntgd', d, pool_w).reshape(n, t, POOL_WIDTH) * pool_scale
    return y, ext[:, -POOL_STATE:]


def token_mixing(h, pos, prev_k, prev_v, prev_pool, w_in, sinks, pool_w, pool_scale, w_out):
    n, t, _ = h.shape
    proj = h @ w_in
    q = rope(proj[..., :ATTN_WIDTH].reshape(n, t, N_KV_HEADS, GROUP, HEAD_DIM), pos)
    k = rope(proj[..., ATTN_WIDTH:ATTN_WIDTH + KV_WIDTH].reshape(n, t, N_KV_HEADS, HEAD_DIM), pos)
    v = proj[..., ATTN_WIDTH + KV_WIDTH:ATTN_WIDTH + 2 * KV_WIDTH].reshape(n, t, N_KV_HEADS, HEAD_DIM)
    u = proj[..., ATTN_WIDTH + 2 * KV_WIDTH:]
    if prev_k is None:
        attn = banded_window_attention(q, k, v, sinks)
        new_k, new_v = k[:, -WINDOW:], v[:, -WINDOW:]
        prev_pool = jnp.zeros((n, POOL_STATE, POOL_WIDTH), u.dtype)
    else:
        attn, new_k, new_v = cached_window_attention(q, k, v, prev_k, prev_v, pos, sinks)
    pooled, new_pool = multiscale_pool(u, prev_pool, pos, pool_w, pool_scale)
    y = jnp.concatenate([attn.reshape(n, t, ATTN_WIDTH), pooled], axis=-1) @ w_out
    return y, (new_k, new_v, new_pool)


def peer_ffn(h, peer_wq, peer_subkeys, peer_u, peer_v):
    n, t, d = h.shape
    tot = n * t
    x = h.reshape(tot, d)
    q = (x @ peer_wq).reshape(tot, PEER_HEADS, 2, PEER_HALF)
    s = jnp.einsum('thpc,hpkc->thpk', q, peer_subkeys).astype(F32)
    top, idx = lax.top_k(s, PEER_TOPK)
    cand = (top[..., 0, :, None] + top[..., 1, None, :]).reshape(tot, PEER_HEADS, PEER_TOPK * PEER_TOPK)
    cidx = (idx[..., 0, :, None] * N_KEYS + idx[..., 1, None, :]).reshape(tot, PEER_HEADS, PEER_TOPK * PEER_TOPK)
    best, sel = lax.top_k(cand, PEER_TOPK)
    eidx = jnp.take_along_axis(cidx, sel, axis=-1).reshape(tot, PEER_HEADS * PEER_TOPK)
    gate = jax.nn.softmax(best, axis=-1).reshape(tot, PEER_HEADS * PEER_TOPK).astype(h.dtype)
    nblk = -(-tot // PEER_BLOCK)
    pad = nblk * PEER_BLOCK - tot
    xb = jnp.pad(x, ((0, pad), (0, 0))).reshape(nblk, PEER_BLOCK, d)
    ib = jnp.pad(eidx, ((0, pad), (0, 0))).reshape(nblk, PEER_BLOCK, PEER_HEADS * PEER_TOPK)
    gb = jnp.pad(gate, ((0, pad), (0, 0))).reshape(nblk, PEER_BLOCK, PEER_HEADS * PEER_TOPK)

    def expert_block(args):
        xk, ik, gk = args
        a = jnp.einsum('tkd,td->tk', peer_u[ik], xk)
        act = jax.nn.gelu(a, approximate=False) * gk
        return jnp.einsum('tk,tkd->td', act, peer_v[ik])

    y = lax.map(expert_block, (xb, ib, gb)).reshape(nblk * PEER_BLOCK, d)[:tot]
    return y.reshape(n, t, d)


def decoder_layer(x, c, pos, prev_k, prev_v, prev_pool, w_ada, b_ada, w_in, sinks, pool_w, pool_scale,
                  w_out, ln1_g, ln1_b, peer_wq, peer_subkeys, peer_u, peer_v, ln2_g, ln2_b):
    sh1, sc1, g1, sh2, sc2, g2 = adaln_modulation(c, w_ada, b_ada)
    y1, state = token_mixing(x * (1 + sc1) + sh1, pos, prev_k, prev_v, prev_pool,
                             w_in, sinks, pool_w, pool_scale, w_out)
    x = layer_norm(ALPHA * x + g1 * y1, ln1_g, ln1_b)
    y2 = peer_ffn(x * (1 + sc2) + sh2, peer_wq, peer_subkeys, peer_u, peer_v)
    x = layer_norm(ALPHA * x + g2 * y2, ln2_g, ln2_b)
    return x, state


def setup_inputs(seed: int = 0) -> dict:
    key = jax.random.key(seed)
    ks = jax.random.split(key, 24)
    win = min(WINDOW, PAST_LEN)

    def nrm(k, shape, scale):
        return jax.random.normal(k, shape, F32) * scale

    return {
        'x_prompt': nrm(ks[0], (BATCH, SEQ, D_MODEL), 1.0),
        'x_sample': nrm(ks[1], (DEC_BATCH, DEC_SEQ, D_MODEL), 1.0),
        'cache_k': nrm(ks[2], (DEPTH, DEC_BATCH, win, N_KV_HEADS, HEAD_DIM), 1.0),
        'cache_v': nrm(ks[3], (DEPTH, DEC_BATCH, win, N_KV_HEADS, HEAD_DIM), 1.0),
        'state_pool': nrm(ks[4], (DEPTH, DEC_BATCH, POOL_STATE, POOL_WIDTH), 1.0),
        'c_prompt': nrm(ks[5], (BATCH, D_MODEL), 1.0),
        'c_sample': nrm(ks[6], (DEC_BATCH, D_MODEL), 1.0),
        'w_ada': nrm(ks[7], (DEPTH, D_MODEL, N_MOD * D_MODEL), 0.5 * D_MODEL ** -0.5),
        'b_ada': nrm(ks[8], (DEPTH, N_MOD * D_MODEL), 0.01),
        'w_in': nrm(ks[9], (DEPTH, D_MODEL, IN_COLS), D_MODEL ** -0.5),
        'sinks': nrm(ks[10], (DEPTH, N_HEADS), 1.0),
        'pool_w': nrm(ks[11], (DEPTH, N_POOL_GROUPS, POOL_GROUP_WIDTH, POOL_GROUP_WIDTH), POOL_GROUP_WIDTH ** -0.5),
        'pool_scale': 1.0 + nrm(ks[12], (DEPTH, POOL_WIDTH), 0.1),
        'w_out': nrm(ks[13], (DEPTH, MIX_WIDTH, D_MODEL), BETA * MIX_WIDTH ** -0.5),
        'ln1_g': 1.0 + nrm(ks[14], (DEPTH, D_MODEL), 0.05),
        'ln1_b': nrm(ks[15], (DEPTH, D_MODEL), 0.02),
        'peer_wq': nrm(ks[16], (DEPTH, D_MODEL, PEER_HEADS * PEER_KEY_DIM), D_MODEL ** -0.5),
        'peer_subkeys': nrm(ks[17], (DEPTH, PEER_HEADS, 2, N_KEYS, PEER_HALF), PEER_HALF ** -0.5),
        'peer_u': nrm(ks[18], (DEPTH, N_EXPERTS, D_MODEL), D_MODEL ** -0.5),
        'peer_v': nrm(ks[19], (DEPTH, N_EXPERTS, D_MODEL), BETA),
        'ln2_g': 1.0 + nrm(ks[20], (DEPTH, D_MODEL), 0.05),
        'ln2_b': nrm(ks[21], (DEPTH, D_MODEL), 0.02),
    }


def reference(x_prompt, x_sample, cache_k, cache_v, state_pool, c_prompt, c_sample, w_ada, b_ada, w_in,
              sinks, pool_w, pool_scale, w_out, ln1_g, ln1_b, peer_wq, peer_subkeys, peer_u, peer_v,
              ln2_g, ln2_b):
    pos_p = jnp.arange(x_prompt.shape[1], dtype=jnp.int32)
    pos_s = PAST_LEN + jnp.arange(x_sample.shape[1], dtype=jnp.int32)
    xp, xs = x_prompt, x_sample
    kp, vp, pp, ksl, vsl, psl = [], [], [], [], [], []
    for l in range(DEPTH):
        lw = (w_ada[l], b_ada[l], w_in[l], sinks[l], pool_w[l], pool_scale[l], w_out[l], ln1_g[l], ln1_b[l],
              peer_wq[l], peer_subkeys[l], peer_u[l], peer_v[l], ln2_g[l], ln2_b[l])
        xp, (k1, v1, p1) = decoder_layer(xp, c_prompt, pos_p, None, None, None, *lw)
        xs, (k2, v2, p2) = decoder_layer(xs, c_sample, pos_s, cache_k[l], cache_v[l], state_pool[l], *lw)
        kp.append(k1)
        vp.append(v1)
        pp.append(p1)
        ksl.append(k2)
        vsl.append(v2)
        psl.append(p2)
    return (xp, xs, jnp.stack(kp), jnp.stack(vp), jnp.stack(pp), jnp.stack(ksl), jnp.stack(vsl), jnp.stack(psl))
```

```python
import functools

import jax
import jax.numpy as jnp
from jax import lax
from jax.experimental import pallas as pl
from jax.experimental.pallas import tpu as pltpu

F32 = jnp.float32
BF16 = jnp.bfloat16

LANES = 128
HEAD_DIM = 64
N_HEADS = 32
N_KV_HEADS = 4
GROUP = N_HEADS // N_KV_HEADS
KV_WIDTH = N_KV_HEADS * HEAD_DIM
WINDOW = 128
ROT_DIM = HEAD_DIM // 4
ROT_HALF = ROT_DIM // 2
ROPE_THETA = 500000.0
POOL_WINDOWS = (2, 4, 8, 16)
POOL_GROUP_WIDTH = 512
POOL_HALO = 16
POOL_STATE = 15
PEER_HEADS = 8
PEER_HALF = 128
N_KEYS = 128
PEER_TOPK = 16
PEER_K = PEER_HEADS * PEER_TOPK
N_MOD = 6
PAST_LEN = 8192
LN_ROWS = 128
POOL_TG = 16
LN_EPS = 1e-5
NEG_INF = -1e30


def _params(vmem_mb, sem=None):
    return pltpu.CompilerParams(dimension_semantics=sem, vmem_limit_bytes=vmem_mb << 20)


def _ada_kernel(c_ref, w_ref, b_ref, o_ref):
    c = c_ref[...]
    s = (c * jax.nn.sigmoid(c)).astype(BF16)
    o_ref[...] = jnp.dot(s, w_ref[...].astype(BF16), preferred_element_type=F32) + b_ref[...]


def adaln(c_all, w_ada, b_ada, tn=512):
    r, d = c_all.shape
    n = w_ada.shape[1]
    return pl.pallas_call(
        _ada_kernel,
        grid=(n // tn,),
        in_specs=[pl.BlockSpec((r, d), lambda j: (0, 0)),
                  pl.BlockSpec((d, tn), lambda j: (0, j)),
                  pl.BlockSpec((1, tn), lambda j: (0, j))],
        out_specs=pl.BlockSpec((r, tn), lambda j: (0, j)),
        out_shape=jax.ShapeDtypeStruct((r, n), F32),
        compiler_params=_params(48, ("arbitrary",)),
        name="adaln",
    )(c_all, w_ada, b_ada)


def _mod_kernel(x_ref, sc_ref, sh_ref, o_ref):
    h = x_ref[...] * (1.0 + sc_ref[...]) + sh_ref[...]
    o_ref[...] = h.reshape(o_ref.shape).astype(o_ref.dtype)


def modulate(x3, sc, sh, tg, tt):
    g, t, d = x3.shape
    nt = t // tt
    return pl.pallas_call(
        _mod_kernel,
        grid=(g // tg, nt),
        in_specs=[pl.BlockSpec((tg, tt, d), lambda a, b: (a, b, 0)),
                  pl.BlockSpec((tg, 1, d), lambda a, b: (a, 0, 0)),
                  pl.BlockSpec((tg, 1, d), lambda a, b: (a, 0, 0))],
        out_specs=pl.BlockSpec((tg * tt, d), lambda a, b: (a * nt + b, 0)),
        out_shape=jax.ShapeDtypeStruct((g * t, d), BF16),
        compiler_params=_params(48, ("arbitrary", "arbitrary")),
        name="modulate",
    )(x3, sc, sh)


def _mm_kernel(*refs, n_pairs):
    o_ref = refs[-1]
    acc = None
    for p in range(n_pairs):
        part = jnp.dot(refs[2 * p][...], refs[2 * p + 1][...], preferred_element_type=F32)
        acc = part if acc is None else acc + part
    o_ref[...] = acc.astype(o_ref.dtype)


def matmul(pairs, tm=512, tn=512, out_dtype=F32, name="matmul"):
    m = pairs[0][0].shape[0]
    n = pairs[0][1].shape[1]
    tm, tn = min(tm, m), min(tn, n)
    in_specs, args = [], []
    for x, w in pairs:
        k = x.shape[1]
        in_specs += [pl.BlockSpec((tm, k), lambda i, j: (i, 0)), pl.BlockSpec((k, tn), lambda i, j: (0, j))]
        args += [x, w]
    return pl.pallas_call(
        functools.partial(_mm_kernel, n_pairs=len(pairs)),
        grid=(m // tm, n // tn),
        in_specs=in_specs,
        out_specs=pl.BlockSpec((tm, tn), lambda i, j: (i, j)),
        out_shape=jax.ShapeDtypeStruct((m, n), out_dtype),
        compiler_params=_params(48, ("arbitrary", "arbitrary")),
        name=name,
    )(*args)


def rope_tables(pos):
    inv_freq = ROPE_THETA ** (-jnp.arange(ROT_HALF, dtype=F32) / ROT_HALF)
    ang = pos.astype(F32)[:, None] * inv_freq[None, :]
    cos, sin = jnp.cos(ang), jnp.sin(ang)
    t = pos.shape[0]
    ones = jnp.ones((t, HEAD_DIM - ROT_DIM), F32)
    zeros_h = jnp.zeros((t, ROT_HALF), F32)
    zeros_r = jnp.zeros((t, HEAD_DIM - ROT_DIM), F32)
    c = jnp.concatenate([cos, cos, ones], axis=1)
    s1 = jnp.concatenate([zeros_h, sin, zeros_r], axis=1)
    s2 = jnp.concatenate([-sin, zeros_h, zeros_r], axis=1)
    rep = LANES // HEAD_DIM
    return tuple(jnp.tile(a, (1, rep)) for a in (c, s1, s2))


def _rope(x, c, s1, s2):
    rows, w = x.shape
    reps = (rows // c.shape[0], w // LANES)
    ct, s1t, s2t = (jnp.tile(a, reps) for a in (c, s1, s2))
    return x * ct + pltpu.roll(x, ROT_HALF, 1) * s1t + pltpu.roll(x, w - ROT_HALF, 1) * s2t


def _sink_softmax_pv(pieces, sink):
    m = sink
    for s, _ in pieces:
        m = jnp.maximum(m, s.max(-1, keepdims=True))
    denom = jnp.exp(sink - m)
    o = None
    for s, v in pieces:
        p = jnp.exp(s - m)
        denom = denom + p.sum(-1, keepdims=True)
        pv = jnp.dot(p.astype(BF16), v, preferred_element_type=F32)
        o = pv if o is None else o + pv
    return o / denom


_NT = (((1,), (1,)), ((), ()))


def _attn_prompt_kernel(q_ref, kp_ref, kc_ref, vp_ref, vc_ref, cc_ref, s1c_ref, s2c_ref,
                        cp_ref, s1p_ref, s2p_ref, sink_ref, o_ref, krot_ref):
    i = pl.program_id(1)
    q = (_rope(q_ref[...], cc_ref[...], s1c_ref[...], s2c_ref[...]) * (HEAD_DIM ** -0.5)).astype(BF16)
    kc = _rope(kc_ref[...], cc_ref[...], s1c_ref[...], s2c_ref[...])
    krot_ref[...] = kc
    kp = _rope(kp_ref[...], cp_ref[...], s1p_ref[...], s2p_ref[...])
    kc, kp = kc.astype(BF16), kp.astype(BF16)
    vc, vp = vc_ref[...].astype(BF16), vp_ref[...].astype(BF16)
    r = lax.broadcasted_iota(jnp.int32, (WINDOW, WINDOW), 0)
    j = lax.broadcasted_iota(jnp.int32, (WINDOW, WINDOW), 1)
    mask_c = j <= r
    mask_p = (j >= r) & (i > 0)
    outs = []
    for hd in range(N_HEADS):
        h = hd // GROUP
        hs = slice(h * HEAD_DIM, (h + 1) * HEAD_DIM)
        qh = q[:, hd * HEAD_DIM:(hd + 1) * HEAD_DIM]
        sp = jnp.where(mask_p, lax.dot_general(qh, kp[:, hs], _NT, preferred_element_type=F32), NEG_INF)
        sc = jnp.where(mask_c, lax.dot_general(qh, kc[:, hs], _NT, preferred_element_type=F32), NEG_INF)
        outs.append(_sink_softmax_pv([(sp, vp[:, hs]), (sc, vc[:, hs])], sink_ref[hd]))
    o_ref[...] = jnp.concatenate(outs, axis=1).astype(o_ref.dtype)


def attn_prompt(q, kv, tabs, sinks, batch, seq):
    nb = seq // WINDOW
    aw = N_HEADS * HEAD_DIM
    cur = lambda b, i: (b * nb + i, 0)
    prev = lambda b, i: (b * nb + jnp.maximum(i - 1, 0), 0)
    cur_v = lambda b, i: (b * nb + i, 1)
    prev_v = lambda b, i: (b * nb + jnp.maximum(i - 1, 0), 1)
    tcur = lambda b, i: (i, 0)
    tprev = lambda b, i: (jnp.maximum(i - 1, 0), 0)
    tspec = lambda f: pl.BlockSpec((WINDOW, LANES), f)
    return pl.pallas_call(
        _attn_prompt_kernel,
        grid=(batch, nb),
        in_specs=[pl.BlockSpec((WINDOW, aw), cur),
                  pl.BlockSpec((WINDOW, KV_WIDTH), prev), pl.BlockSpec((WINDOW, KV_WIDTH), cur),
                  pl.BlockSpec((WINDOW, KV_WIDTH), prev_v), pl.BlockSpec((WINDOW, KV_WIDTH), cur_v),
                  tspec(tcur), tspec(tcur), tspec(tcur), tspec(tprev), tspec(tprev), tspec(tprev),
                  pl.BlockSpec(memory_space=pltpu.SMEM)],
        out_specs=[pl.BlockSpec((WINDOW, aw), cur), pl.BlockSpec((WINDOW, KV_WIDTH), cur)],
        out_shape=[jax.ShapeDtypeStruct((batch * seq, aw), BF16),
                   jax.ShapeDtypeStruct((batch * seq, KV_WIDTH), F32)],
        compiler_params=_params(32, ("arbitrary", "arbitrary")),
        name="attn_prompt",
    )(q, kv, kv, kv, kv, *tabs, *tabs, sinks)


def _attn_sample_kernel(q_ref, kv_ref, ck_ref, cv_ref, c_ref, s1_ref, s2_ref, sink_ref, o_ref, krot_ref,
                        qs_ref, os_ref, *, nb, t_new):
    tabs = (c_ref[...], s1_ref[...], s2_ref[...])
    qs_ref[...] = (_rope(q_ref[...], *tabs) * (HEAD_DIM ** -0.5)).astype(BF16).astype(F32)
    krot_ref[...] = _rope(kv_ref[:, :KV_WIDTH], *tabs)
    wc = ck_ref.shape[1]
    r = lax.broadcasted_iota(jnp.int32, (t_new, wc), 0)
    j = lax.broadcasted_iota(jnp.int32, (t_new, wc), 1)
    diff_c = r + wc - j
    mask_c = (diff_c >= 0) & (diff_c <= WINDOW)
    rn = lax.broadcasted_iota(jnp.int32, (t_new, t_new), 0)
    jn = lax.broadcasted_iota(jnp.int32, (t_new, t_new), 1)
    mask_n = jn <= rn

    def body(n, carry):
        rows = pl.ds(pl.multiple_of(n * t_new, t_new), t_new)
        q = qs_ref[rows, :].astype(BF16)
        kn = krot_ref[rows, :].astype(BF16)
        vn = kv_ref[rows, KV_WIDTH:].astype(BF16)
        kc = ck_ref[n].astype(BF16)
        vc = cv_ref[n].astype(BF16)
        outs = []
        for hd in range(N_HEADS):
            h = hd // GROUP
            hs = slice(h * HEAD_DIM, (h + 1) * HEAD_DIM)
            qh = q[:, hd * HEAD_DIM:(hd + 1) * HEAD_DIM]
            sc = jnp.where(mask_c, lax.dot_general(qh, kc[:, hs], _NT, preferred_element_type=F32), NEG_INF)
            sn = jnp.where(mask_n, lax.dot_general(qh, kn[:, hs], _NT, preferred_element_type=F32), NEG_INF)
            outs.append(_sink_softmax_pv([(sc, vc[:, hs]), (sn, vn[:, hs])], sink_ref[hd]))
        os_ref[rows, :] = jnp.concatenate(outs, axis=1)
        return carry

    lax.fori_loop(0, nb, body, 0)
    o_ref[...] = os_ref[...].astype(o_ref.dtype)


def attn_sample(q, kv, cache_k, cache_v, tabs, sinks, t_new, nb=8):
    rows = q.shape[0]
    n = rows // t_new
    wc = cache_k.shape[1]
    aw = N_HEADS * HEAD_DIM
    tm = nb * t_new
    blk = lambda i: (i, 0)
    full = lambda i: (0, 0)
    return pl.pallas_call(
        functools.partial(_attn_sample_kernel, nb=nb, t_new=t_new),
        grid=(n // nb,),
        in_specs=[pl.BlockSpec((tm, aw), blk), pl.BlockSpec((tm, 2 * KV_WIDTH), blk),
                  pl.BlockSpec((nb, wc, KV_WIDTH), lambda i: (i, 0, 0)),
                  pl.BlockSpec((nb, wc, KV_WIDTH), lambda i: (i, 0, 0)),
                  pl.BlockSpec((t_new, LANES), full), pl.BlockSpec((t_new, LANES), full),
                  pl.BlockSpec((t_new, LANES), full),
                  pl.BlockSpec(memory_space=pltpu.SMEM)],
        out_specs=[pl.BlockSpec((tm, aw), blk), pl.BlockSpec((tm, KV_WIDTH), blk)],
        out_shape=[jax.ShapeDtypeStruct((rows, aw), BF16), jax.ShapeDtypeStruct((rows, KV_WIDTH), F32)],
        scratch_shapes=[pltpu.VMEM((tm, aw), F32), pltpu.VMEM((tm, aw), F32)],
        compiler_params=_params(32, ("arbitrary",)),
        name="attn_sample",
    )(q, kv, cache_k, cache_v, *tabs, sinks)


def _pool_kernel(u_ref, halo_ref, st_ref, pw_ref, scale_ref, o_ref, ext_ref, *, pos0):
    ti = pl.program_id(1)
    tg, tt, _ = u_ref.shape
    ext_ref[:, 0:POOL_HALO, :] = jnp.where(ti == 0, st_ref[...], halo_ref[...])
    ext_ref[:, POOL_HALO:, :] = u_ref[...]
    pos = pos0 + ti * tt + lax.broadcasted_iota(jnp.int32, (1, tt, 1), 1)
    for g, w in enumerate(POOL_WINDOWS):
        cols = slice(g * POOL_GROUP_WIDTH, (g + 1) * POOL_GROUP_WIDTH)
        acc = ext_ref[:, POOL_HALO:POOL_HALO + tt, cols]
        for k in range(1, w):
            acc = acc + ext_ref[:, POOL_HALO - k:POOL_HALO - k + tt, cols]
        inv_cnt = 1.0 / jnp.minimum(w, pos + 1).astype(F32)
        d = acc * inv_cnt - u_ref[:, :, cols]
        d = d.reshape(tg * tt, POOL_GROUP_WIDTH).astype(BF16)
        y = jnp.dot(d, pw_ref[g], preferred_element_type=F32) * scale_ref[:, cols]
        o_ref[:, cols] = y.astype(o_ref.dtype)


def pool_mix(u3, state16, pool_w, pool_scale, tg, tt, pos0):
    g, t, pw = u3.shape
    nt = t // tt
    hb = tt // POOL_HALO
    halo_src = u3 if nt > 1 else state16
    return pl.pallas_call(
        functools.partial(_pool_kernel, pos0=pos0),
        grid=(g // tg, nt),
        in_specs=[pl.BlockSpec((tg, tt, pw), lambda a, b: (a, b, 0)),
                  pl.BlockSpec((tg, POOL_HALO, pw), lambda a, b: (a, jnp.maximum(b * hb - 1, 0), 0)),
                  pl.BlockSpec((tg, POOL_HALO, pw), lambda a, b: (a, 0, 0)),
                  pl.BlockSpec(pool_w.shape, lambda a, b: (0, 0, 0)),
                  pl.BlockSpec((1, pw), lambda a, b: (0, 0))],
        out_specs=pl.BlockSpec((tg * tt, pw), lambda a, b: (a * nt + b, 0)),
        out_shape=jax.ShapeDtypeStruct((g * t, pw), BF16),
        scratch_shapes=[pltpu.VMEM((tg, POOL_HALO + tt, pw), F32)],
        compiler_params=_params(48, ("arbitrary", "arbitrary")),
        name="pool_mix",
    )(u3, halo_src, state16, pool_w, pool_scale)


def _layer_norm(z, g, b):
    mu = z.mean(-1, keepdims=True)
    zc = z - mu
    var = (zc * zc).mean(-1, keepdims=True)
    return zc * lax.rsqrt(var + LN_EPS) * g + b


def _ln1_kernel(x_ref, y_ref, g1_ref, sc_ref, sh_ref, lg_ref, lb_ref, x1_ref, h_ref, hb_ref, *, alpha):
    x = x_ref[...]
    z = alpha * x + g1_ref[...] * y_ref[...].reshape(x.shape)
    x1 = _layer_norm(z, lg_ref[...], lb_ref[...])
    h = x1 * (1.0 + sc_ref[...]) + sh_ref[...]
    x1_ref[...] = x1.reshape(x1_ref.shape)
    h2 = h.reshape(h_ref.shape)
    h_ref[...] = h2
    hb_ref[...] = h2.astype(BF16)


def ln1_modulate(x3, y, g1, sc, sh, ln_g, ln_b, tg, tt, alpha):
    g, t, d = x3.shape
    nt = t // tt
    rows = lambda a, b: (a * nt + b, 0)
    grp = lambda a, b: (a, 0, 0)
    vec = lambda a, b: (0, 0)
    return pl.pallas_call(
        functools.partial(_ln1_kernel, alpha=alpha),
        grid=(g // tg, nt),
        in_specs=[pl.BlockSpec((tg, tt, d), lambda a, b: (a, b, 0)), pl.BlockSpec((tg * tt, d), rows),
                  pl.BlockSpec((tg, 1, d), grp), pl.BlockSpec((tg, 1, d), grp), pl.BlockSpec((tg, 1, d), grp),
                  pl.BlockSpec((1, d), vec), pl.BlockSpec((1, d), vec)],
        out_specs=[pl.BlockSpec((tg * tt, d), rows)] * 3,
        out_shape=[jax.ShapeDtypeStruct((g * t, d), F32), jax.ShapeDtypeStruct((g * t, d), F32),
                   jax.ShapeDtypeStruct((g * t, d), BF16)],
        compiler_params=_params(48, ("arbitrary", "arbitrary")),
        name="ln1_modulate",
    )(x3, y, g1, sc, sh, ln_g, ln_b)


def _ln2_kernel(x_ref, y_ref, g2_ref, lg_ref, lb_ref, o_ref, *, alpha):
    z = alpha * x_ref[...] + g2_ref[...] * y_ref[...]
    o_ref[...] = _layer_norm(z, lg_ref[...], lb_ref[...])


def ln2(x3, y3, g2, ln_g, ln_b, tg, tt, alpha):
    g, t, d = x3.shape
    blk = pl.BlockSpec((tg, tt, d), lambda a, b: (a, b, 0))
    vec = pl.BlockSpec((1, d), lambda a, b: (0, 0))
    return pl.pallas_call(
        functools.partial(_ln2_kernel, alpha=alpha),
        grid=(g // tg, t // tt),
        in_specs=[blk, blk, pl.BlockSpec((tg, 1, d), lambda a, b: (a, 0, 0)), vec, vec],
        out_specs=blk,
        out_shape=jax.ShapeDtypeStruct((g, t, d), F32),
        compiler_params=_params(48, ("arbitrary", "arbitrary")),
        name="ln2",
    )(x3, y3, g2, ln_g, ln_b)


def _topk_rows(vals, k, payload=None):
    n = vals.shape[0]
    iota = lax.broadcasted_iota(jnp.int32, vals.shape, 0).astype(F32)
    out_v, out_i = [], []
    for _ in range(k):
        m = jnp.max(vals, axis=0, keepdims=True)
        am = jnp.min(jnp.where(vals == m, iota, float(n)), axis=0, keepdims=True)
        hit = iota == am
        out_v.append(m)
        out_i.append(am if payload is None else jnp.sum(jnp.where(hit, payload, 0.0), axis=0, keepdims=True))
        vals = jnp.where(hit, -jnp.inf, vals)
    return jnp.concatenate(out_v, axis=0), jnp.concatenate(out_i, axis=0)


def _peer_select_kernel(q_ref, sk_ref, idx_ref, gate_ref):
    tm = q_ref.shape[0]
    idx_rows, gate_rows = [], []
    for h in range(PEER_HEADS):
        tops = []
        for p in range(2):
            c0 = (h * 2 + p) * PEER_HALF
            qhp = q_ref[:, c0:c0 + PEER_HALF].astype(BF16)
            s_t = lax.dot_general(sk_ref[h, p], qhp, _NT, preferred_element_type=F32)
            tops.append(_topk_rows(s_t, PEER_TOPK))
        (v1, i1), (v2, i2) = tops
        cand = jnp.concatenate([v1[a:a + 1, :] + v2 for a in range(PEER_TOPK)], axis=0)
        cidx = jnp.concatenate([i1[a:a + 1, :] * float(N_KEYS) + i2 for a in range(PEER_TOPK)], axis=0)
        best, eidx = _topk_rows(cand, PEER_TOPK, payload=cidx)
        e = jnp.exp(best - best[0:1, :])
        gate_rows.append(e / e.sum(axis=0, keepdims=True))
        idx_rows.append(eidx)
    idx_ref[...] = jnp.concatenate(idx_rows, axis=0).T.astype(jnp.int32)
    gate_ref[...] = jnp.concatenate(gate_rows, axis=0).T


def peer_select(qp, subkeys_bf16, tm=128):
    rows, w = qp.shape
    tm = min(tm, rows)
    blk = lambda i: (i, 0)
    return pl.pallas_call(
        _peer_select_kernel,
        grid=(rows // tm,),
        in_specs=[pl.BlockSpec((tm, w), blk), pl.BlockSpec(subkeys_bf16.shape, lambda i: (0, 0, 0, 0))],
        out_specs=[pl.BlockSpec((tm, PEER_K), blk), pl.BlockSpec((tm, PEER_K), blk)],
        out_shape=[jax.ShapeDtypeStruct((rows, PEER_K), jnp.int32), jax.ShapeDtypeStruct((rows, PEER_K), F32)],
        compiler_params=_params(32, ("arbitrary",)),
        name="peer_select",
    )(qp, subkeys_bf16)


def pack_expert_tables(peer_u, peer_v):
    half = peer_u.shape[1] // 2

    def pack(t):
        b = lax.bitcast_convert_type(t.astype(BF16), jnp.uint16).astype(jnp.uint32)
        return b[:, :half] | (b[:, half:] << 16)

    return jnp.concatenate([pack(peer_u), pack(peer_v)], axis=1)


def _unpack(w):
    lo = lax.bitcast_convert_type(w << 16, F32)
    hi = lax.bitcast_convert_type(w & jnp.uint32(0xFFFF0000), F32)
    return lo, hi


def _peer_mix_kernel(idx_ref, nidx_ref, x_ref, gate_ref, tab_ref, y_ref, buf, sem, *, tt, nsteps):
    i = pl.program_id(0)
    slot = i % 2
    half = x_ref.shape[1] // 2
    nchunk = half // LANES

    def row_copy(e, slot_, t, k):
        return pltpu.make_async_copy(tab_ref.at[pl.ds(e, 1), :], buf.at[slot_, t, pl.ds(k, 1), :], sem.at[slot_, t])

    def issue(ids, slot_, t):
        for k in range(PEER_K):
            row_copy(ids[t, k], slot_, t, k).start()

    @pl.when(i == 0)
    def _():
        def first(t, c):
            issue(idx_ref, 0, t)
            return c
        lax.fori_loop(0, tt, first, 0)

    eye = (lax.broadcasted_iota(jnp.int32, (PEER_K, PEER_K), 0)
           == lax.broadcasted_iota(jnp.int32, (PEER_K, PEER_K), 1))
    row_id = lax.broadcasted_iota(jnp.int32, (tt, LANES), 0)
    y_ref[...] = jnp.zeros(y_ref.shape, F32)

    def token(t, c):
        @pl.when(i + 1 < nsteps)
        def _():
            issue(nidx_ref, 1 - slot, t)

        pltpu.make_async_copy(tab_ref.at[pl.ds(0, PEER_K), :], buf.at[slot, t], sem.at[slot, t]).wait()
        x = x_ref[pl.ds(t, 1), :]
        acc = jnp.zeros((PEER_K, LANES), F32)
        for ch in range(nchunk):
            cs = slice(ch * LANES, (ch + 1) * LANES)
            lo, hi = _unpack(buf[slot, t, :, cs])
            acc = acc + lo * x[:, cs] + hi * x[:, half + ch * LANES:half + (ch + 1) * LANES]
        a = acc.sum(axis=1, keepdims=True)
        gate = jnp.sum(jnp.where(eye, gate_ref[pl.ds(t, 1), :], 0.0), axis=1, keepdims=True)
        act = 0.5 * a * (1.0 + lax.erf(a * (2.0 ** -0.5))) * gate
        act_b = jnp.broadcast_to(act, (PEER_K, LANES))
        mine = row_id == t
        for ch in range(nchunk):
            lo, hi = _unpack(buf[slot, t, :, half + ch * LANES:half + (ch + 1) * LANES])
            for part, c0 in ((lo, ch * LANES), (hi, half + ch * LANES)):
                cs = slice(c0, c0 + LANES)
                y_ref[:, cs] = jnp.where(mine, (act_b * part).sum(axis=0, keepdims=True), y_ref[:, cs])
        return c

    lax.fori_loop(0, tt, token, 0)


def peer_mix(h2, eidx, gate, table, tt=8):
    rows, d = h2.shape
    nsteps = rows // tt
    blk = lambda i: (i, 0)
    nxt = lambda i: (jnp.minimum(i + 1, nsteps - 1), 0)
    return pl.pallas_call(
        functools.partial(_peer_mix_kernel, tt=tt, nsteps=nsteps),
        grid=(nsteps,),
        in_specs=[pl.BlockSpec((tt, PEER_K), blk, memory_space=pltpu.SMEM),
                  pl.BlockSpec((tt, PEER_K), nxt, memory_space=pltpu.SMEM),
                  pl.BlockSpec((tt, d), blk), pl.BlockSpec((tt, PEER_K), blk),
                  pl.BlockSpec(memory_space=pl.ANY)],
        out_specs=pl.BlockSpec((tt, d), blk),
        out_shape=jax.ShapeDtypeStruct((rows, d), F32),
        scratch_shapes=[pltpu.VMEM((2, tt, PEER_K, d), jnp.uint32), pltpu.SemaphoreType.DMA((2, tt))],
        compiler_params=_params(56, ("arbitrary",)),
        name="peer_mix",
    )(eidx, eidx, h2, gate, table)


def _layer(x3, mods, lw, pos0, tg, tt, prev=None):
    g, t, d = x3.shape
    sh1, sc1, g1, sh2, sc2, g2 = mods
    aw = N_HEADS * HEAD_DIM
    alpha = lw["alpha"]

    h = modulate(x3, sc1, sh1, tg, tt)
    q = matmul([(h, lw["w_q"])], name="in_q")
    kv = matmul([(h, lw["w_kv"])], name="in_kv")
    u = matmul([(h, lw["w_u"])], name="in_u")

    tabs = rope_tables(pos0 + jnp.arange(t, dtype=jnp.int32))
    if prev is None:
        attn, krot = attn_prompt(q, kv, tabs, lw["sinks"], g, t)
        state16 = jnp.zeros((g, POOL_HALO, u.shape[1]), F32)
        new_k = krot.reshape(g, t, N_KV_HEADS, HEAD_DIM)[:, -WINDOW:]
        new_v = kv[:, KV_WIDTH:].reshape(g, t, N_KV_HEADS, HEAD_DIM)[:, -WINDOW:]
    else:
        cache_k, cache_v, state = prev
        wc = cache_k.shape[1]
        attn, krot = attn_sample(q, kv, cache_k.reshape(g, wc, KV_WIDTH), cache_v.reshape(g, wc, KV_WIDTH),
                                 tabs, lw["sinks"], t)
        state16 = jnp.pad(state, ((0, 0), (POOL_HALO - POOL_STATE, 0), (0, 0)))
        new_k = jnp.concatenate([cache_k, krot.reshape(g, t, N_KV_HEADS, HEAD_DIM)], axis=1)[:, -wc:]
        new_v = jnp.concatenate([cache_v, kv[:, KV_WIDTH:].reshape(g, t, N_KV_HEADS, HEAD_DIM)], axis=1)[:, -wc:]
    u3 = u.reshape(g, t, u.shape[1])
    new_pool = jnp.concatenate([state16, u3], axis=1)[:, -POOL_STATE:]
    pooled = pool_mix(u3, state16, lw["pool_w"], lw["pool_scale"], min(tg, POOL_TG), tt, pos0)

    y1 = matmul([(attn, lw["w_out_a"]), (pooled, lw["w_out_p"])], name="out_proj")
    ln_tt = min(t, LN_ROWS)
    ln_tg = LN_ROWS // ln_tt
    x1, h2, h2b = ln1_modulate(x3, y1, g1, sc2, sh2, lw["ln1_g"], lw["ln1_b"], ln_tg, ln_tt, alpha)

    qp = matmul([(h2b, lw["peer_wq"])], name="peer_query")
    eidx, gate = peer_select(qp, lw["subkeys"])
    y2 = peer_mix(h2, eidx, gate, lw["table"])
    out = ln2(x1.reshape(g, t, d), y2.reshape(g, t, d), g2, lw["ln2_g"], lw["ln2_b"], ln_tg, ln_tt, alpha)
    return out, (new_k, new_v, new_pool)


def kernel(x_prompt, x_sample, cache_k, cache_v, state_pool, c_prompt, c_sample, w_ada, b_ada, w_in, sinks,
           pool_w, pool_scale, w_out, ln1_g, ln1_b, peer_wq, peer_subkeys, peer_u, peer_v, ln2_g, ln2_b):
    depth = w_ada.shape[0]
    bp, seq, d = x_prompt.shape
    bs, t_new, _ = x_sample.shape
    past_len = PAST_LEN
    alpha = (2 * depth) ** 0.25
    aw = N_HEADS * HEAD_DIM
    n_c = bp + bs
    pad = (-n_c) % 8

    xp, xs = x_prompt, x_sample
    outs = [[] for _ in range(6)]
    for l in range(depth):
        c_all = jnp.concatenate([c_prompt, c_sample, jnp.zeros((pad, d), F32)], axis=0)
        mod = adaln(c_all, w_ada[l], b_ada[l][None, :])
        mod_p = mod[:bp].reshape(bp, N_MOD, 1, d)
        mod_s = mod[bp:n_c].reshape(bs, N_MOD, 1, d)
        mods_p = tuple(mod_p[:, i] for i in range(N_MOD))
        mods_s = tuple(mod_s[:, i] for i in range(N_MOD))
        w_in_b = w_in[l].astype(BF16)
        w_out_b = w_out[l].astype(BF16)
        lw = dict(
            alpha=alpha,
            w_q=w_in_b[:, :aw], w_kv=w_in_b[:, aw:aw + 2 * KV_WIDTH], w_u=w_in_b[:, aw + 2 * KV_WIDTH:],
            sinks=sinks[l], pool_w=pool_w[l].astype(BF16), pool_scale=pool_scale[l][None, :],
            w_out_a=w_out_b[:aw], w_out_p=w_out_b[aw:],
            ln1_g=ln1_g[l][None, :], ln1_b=ln1_b[l][None, :], ln2_g=ln2_g[l][None, :], ln2_b=ln2_b[l][None, :],
            peer_wq=peer_wq[l].astype(BF16), subkeys=peer_subkeys[l].astype(BF16),
            table=pack_expert_tables(peer_u[l], peer_v[l]),
        )
        xp, (k1, v1, p1) = _layer(xp, mods_p, lw, 0, 1, 512)
        xs, (k2, v2, p2) = _layer(xs, mods_s, lw, past_len, 64, t_new, prev=(cache_k[l], cache_v[l], state_pool[l]))
        for lst, val in zip(outs, (k1, v1, p1, k2, v2, p2)):
            lst.append(val)
    return (xp, xs) + tuple(jnp.stack(o) for o in outs)
```

```python
import functools

import jax
import jax.numpy as jnp
from jax import lax
from jax.experimental import pallas as pl
from jax.experimental.pallas import tpu as pltpu

F32 = jnp.float32
BF16 = jnp.bfloat16

LANES = 128
HEAD_DIM = 64
N_HEADS = 32
N_KV_HEADS = 4
GROUP = N_HEADS // N_KV_HEADS
KV_WIDTH = N_KV_HEADS * HEAD_DIM
WINDOW = 128
ROT_DIM = HEAD_DIM // 4
ROT_HALF = ROT_DIM // 2
ROPE_THETA = 500000.0
POOL_WINDOWS = (2, 4, 8, 16)
POOL_GROUP_WIDTH = 512
POOL_HALO = 16
POOL_STATE = 15
PEER_HEADS = 8
PEER_HALF = 128
N_KEYS = 128
PEER_TOPK = 16
PEER_K = PEER_HEADS * PEER_TOPK
N_MOD = 6
PAST_LEN = 8192
LN_ROWS = 128
POOL_TG = 16
LN_EPS = 1e-5
NEG_INF = -1e30


def _params(vmem_mb, sem=None):
    return pltpu.CompilerParams(dimension_semantics=sem, vmem_limit_bytes=vmem_mb << 20)


def _ada_kernel(c_ref, w_ref, b_ref, o_ref):
    c = c_ref[...]
    s = (c * jax.nn.sigmoid(c)).astype(BF16)
    o_ref[...] = jnp.dot(s, w_ref[...].astype(BF16), preferred_element_type=F32) + b_ref[...]


def adaln(c_all, w_ada, b_ada, tn=512):
    r, d = c_all.shape
    n = w_ada.shape[1]
    return pl.pallas_call(
        _ada_kernel,
        grid=(n // tn,),
        in_specs=[pl.BlockSpec((r, d), lambda j: (0, 0)),
                  pl.BlockSpec((d, tn), lambda j: (0, j)),
                  pl.BlockSpec((1, tn), lambda j: (0, j))],
        out_specs=pl.BlockSpec((r, tn), lambda j: (0, j)),
        out_shape=jax.ShapeDtypeStruct((r, n), F32),
        compiler_params=_params(48, ("arbitrary",)),
        name="adaln",
    )(c_all, w_ada, b_ada)


def _mod_kernel(x_ref, sc_ref, sh_ref, o_ref):
    h = x_ref[...] * (1.0 + sc_ref[...]) + sh_ref[...]
    o_ref[...] = h.reshape(o_ref.shape).astype(o_ref.dtype)


def modulate(x3, sc, sh, tg, tt):
    g, t, d = x3.shape
    nt = t // tt
    return pl.pallas_call(
        _mod_kernel,
        grid=(g // tg, nt),
        in_specs=[pl.BlockSpec((tg, tt, d), lambda a, b: (a, b, 0)),
                  pl.BlockSpec((tg, 1, d), lambda a, b: (a, 0, 0)),
                  pl.BlockSpec((tg, 1, d), lambda a, b: (a, 0, 0))],
        out_specs=pl.BlockSpec((tg * tt, d), lambda a, b: (a * nt + b, 0)),
        out_shape=jax.ShapeDtypeStruct((g * t, d), BF16),
        compiler_params=_params(48, ("arbitrary", "arbitrary")),
        name="modulate",
    )(x3, sc, sh)


def _mm_kernel(*refs, n_pairs):
    o_ref = refs[-1]
    acc = None
    for p in range(n_pairs):
        part = jnp.dot(refs[2 * p][...], refs[2 * p + 1][...], preferred_element_type=F32)
        acc = part if acc is None else acc + part
    o_ref[...] = acc.astype(o_ref.dtype)


def matmul(pairs, tm=512, tn=512, out_dtype=F32, name="matmul"):
    m = pairs[0][0].shape[0]
    n = pairs[0][1].shape[1]
    tm, tn = min(tm, m), min(tn, n)
    in_specs, args = [], []
    for x, w in pairs:
        k = x.shape[1]
        in_specs += [pl.BlockSpec((tm, k), lambda i, j: (i, 0)), pl.BlockSpec((k, tn), lambda i, j: (0, j))]
        args += [x, w]
    return pl.pallas_call(
        functools.partial(_mm_kernel, n_pairs=len(pairs)),
        grid=(m // tm, n // tn),
        in_specs=in_specs,
        out_specs=pl.BlockSpec((tm, tn), lambda i, j: (i, j)),
        out_shape=jax.ShapeDtypeStruct((m, n), out_dtype),
        compiler_params=_params(48, ("arbitrary", "arbitrary")),
        name=name,
    )(*args)


def rope_tables(pos):
    inv_freq = ROPE_THETA ** (-jnp.arange(ROT_HALF, dtype=F32) / ROT_HALF)
    ang = pos.astype(F32)[:, None] * inv_freq[None, :]
    cos, sin = jnp.cos(ang), jnp.sin(ang)
    t = pos.shape[0]
    ones = jnp.ones((t, HEAD_DIM - ROT_DIM), F32)
    zeros_h = jnp.zeros((t, ROT_HALF), F32)
    zeros_r = jnp.zeros((t, HEAD_DIM - ROT_DIM), F32)
    c = jnp.concatenate([cos, cos, ones], axis=1)
    s1 = jnp.concatenate([zeros_h, sin, zeros_r], axis=1)
    s2 = jnp.concatenate([-sin, zeros_h, zeros_r], axis=1)
    rep = LANES // HEAD_DIM
    return tuple(jnp.tile(a, (1, rep)) for a in (c, s1, s2))


def _rope(x, c, s1, s2):
    rows, w = x.shape
    reps = (rows // c.shape[0], w // LANES)
    ct, s1t, s2t = (jnp.tile(a, reps) for a in (c, s1, s2))
    return x * ct + pltpu.roll(x, ROT_HALF, 1) * s1t + pltpu.roll(x, w - ROT_HALF, 1) * s2t


def _sink_softmax_pv(pieces, sink):
    m = sink
    for s, _ in pieces:
        m = jnp.maximum(m, s.max(-1, keepdims=True))
    denom = jnp.exp(sink - m)
    o = None
    for s, v in pieces:
        p = jnp.exp(s - m)
        denom = denom + p.sum(-1, keepdims=True)
        pv = jnp.dot(p.astype(BF16), v, preferred_element_type=F32)
        o = pv if o is None else o + pv
    return o / denom


_NT = (((1,), (1,)), ((), ()))


def _attn_prompt_kernel(q_ref, kp_ref, kc_ref, vp_ref, vc_ref, cc_ref, s1c_ref, s2c_ref,
                        cp_ref, s1p_ref, s2p_ref, sink_ref, o_ref, krot_ref):
    i = pl.program_id(1)
    q = (_rope(q_ref[...], cc_ref[...], s1c_ref[...], s2c_ref[...]) * (HEAD_DIM ** -0.5)).astype(BF16)
    kc = _rope(kc_ref[...], cc_ref[...], s1c_ref[...], s2c_ref[...])
    krot_ref[...] = kc
    kp = _rope(kp_ref[...], cp_ref[...], s1p_ref[...], s2p_ref[...])
    kc, kp = kc.astype(BF16), kp.astype(BF16)
    vc, vp = vc_ref[...].astype(BF16), vp_ref[...].astype(BF16)
    r = lax.broadcasted_iota(jnp.int32, (WINDOW, WINDOW), 0)
    j = lax.broadcasted_iota(jnp.int32, (WINDOW, WINDOW), 1)
    mask_c = j <= r
    mask_p = (j >= r) & (i > 0)
    outs = []
    for hd in range(N_HEADS):
        h = hd // GROUP
        hs = slice(h * HEAD_DIM, (h + 1) * HEAD_DIM)
        qh = q[:, hd * HEAD_DIM:(hd + 1) * HEAD_DIM]
        sp = jnp.where(mask_p, lax.dot_general(qh, kp[:, hs], _NT, preferred_element_type=F32), NEG_INF)
        sc = jnp.where(mask_c, lax.dot_general(qh, kc[:, hs], _NT, preferred_element_type=F32), NEG_INF)
        outs.append(_sink_softmax_pv([(sp, vp[:, hs]), (sc, vc[:, hs])], sink_ref[hd]))
    o_ref[...] = jnp.concatenate(outs, axis=1).astype(o_ref.dtype)


def attn_prompt(q, kv, tabs, sinks, batch, seq):
    nb = seq // WINDOW
    aw = N_HEADS * HEAD_DIM
    cur = lambda b, i: (b * nb + i, 0)
    prev = lambda b, i: (b * nb + jnp.maximum(i - 1, 0), 0)
    cur_v = lambda b, i: (b * nb + i, 1)
    prev_v = lambda b, i: (b * nb + jnp.maximum(i - 1, 0), 1)
    tcur = lambda b, i: (i, 0)
    tprev = lambda b, i: (jnp.maximum(i - 1, 0), 0)
    tspec = lambda f: pl.BlockSpec((WINDOW, LANES), f)
    return pl.pallas_call(
        _attn_prompt_kernel,
        grid=(batch, nb),
        in_specs=[pl.BlockSpec((WINDOW, aw), cur),
                  pl.BlockSpec((WINDOW, KV_WIDTH), prev), pl.BlockSpec((WINDOW, KV_WIDTH), cur),
                  pl.BlockSpec((WINDOW, KV_WIDTH), prev_v), pl.BlockSpec((WINDOW, KV_WIDTH), cur_v),
                  tspec(tcur), tspec(tcur), tspec(tcur), tspec(tprev), tspec(tprev), tspec(tprev),
                  pl.BlockSpec(memory_space=pltpu.SMEM)],
        out_specs=[pl.BlockSpec((WINDOW, aw), cur), pl.BlockSpec((WINDOW, KV_WIDTH), cur)],
        out_shape=[jax.ShapeDtypeStruct((batch * seq, aw), BF16),
                   jax.ShapeDtypeStruct((batch * seq, KV_WIDTH), F32)],
        compiler_params=_params(32, ("arbitrary", "arbitrary")),
        name="attn_prompt",
    )(q, kv, kv, kv, kv, *tabs, *tabs, sinks)


def _attn_sample_kernel(q_ref, kv_ref, ck_ref, cv_ref, c_ref, s1_ref, s2_ref, sink_ref, o_ref, krot_ref,
                        qs_ref, os_ref, *, nb, t_new):
    tabs = (c_ref[...], s1_ref[...], s2_ref[...])
    qs_ref[...] = (_rope(q_ref[...], *tabs) * (HEAD_DIM ** -0.5)).astype(BF16).astype(F32)
    krot_ref[...] = _rope(kv_ref[:, :KV_WIDTH], *tabs)
    wc = ck_ref.shape[1]
    gr = GROUP * t_new
    tok = lax.broadcasted_iota(jnp.int32, (gr, wc), 0) % t_new
    diff_c = tok + wc - lax.broadcasted_iota(jnp.int32, (gr, wc), 1)
    mask_c = (diff_c >= 0) & (diff_c <= WINDOW)
    mask_n = (lax.broadcasted_iota(jnp.int32, (gr, t_new), 1)
              <= lax.broadcasted_iota(jnp.int32, (gr, t_new), 0) % t_new)
    sink_cols = [jnp.concatenate([jnp.full((t_new, 1), sink_ref[h * GROUP + g], F32) for g in range(GROUP)], axis=0)
                 for h in range(N_KV_HEADS)]

    def body(n, carry):
        rows = pl.ds(pl.multiple_of(n * t_new, t_new), t_new)
        q = qs_ref[rows, :]
        kn = krot_ref[rows, :].astype(BF16)
        vn = kv_ref[rows, KV_WIDTH:].astype(BF16)
        kc = ck_ref[n].astype(BF16)
        vc = cv_ref[n].astype(BF16)
        outs = []
        for h in range(N_KV_HEADS):
            hs = slice(h * HEAD_DIM, (h + 1) * HEAD_DIM)
            qg = jnp.concatenate([q[:, (h * GROUP + g) * HEAD_DIM:(h * GROUP + g + 1) * HEAD_DIM]
                                  for g in range(GROUP)], axis=0).astype(BF16)
            sc = jnp.where(mask_c, lax.dot_general(qg, kc[:, hs], _NT, preferred_element_type=F32), NEG_INF)
            sn = jnp.where(mask_n, lax.dot_general(qg, kn[:, hs], _NT, preferred_element_type=F32), NEG_INF)
            o = _sink_softmax_pv([(sc, vc[:, hs]), (sn, vn[:, hs])], sink_cols[h])
            outs += [o[g * t_new:(g + 1) * t_new, :] for g in range(GROUP)]
        os_ref[rows, :] = jnp.concatenate(outs, axis=1)
        return carry

    lax.fori_loop(0, nb, body, 0)
    o_ref[...] = os_ref[...].astype(o_ref.dtype)


def attn_sample(q, kv, cache_k, cache_v, tabs, sinks, t_new, nb=8):
    rows = q.shape[0]
    n = rows // t_new
    wc = cache_k.shape[1]
    aw = N_HEADS * HEAD_DIM
    tm = nb * t_new
    blk = lambda i: (i, 0)
    full = lambda i: (0, 0)
    return pl.pallas_call(
        functools.partial(_attn_sample_kernel, nb=nb, t_new=t_new),
        grid=(n // nb,),
        in_specs=[pl.BlockSpec((tm, aw), blk), pl.BlockSpec((tm, 2 * KV_WIDTH), blk),
                  pl.BlockSpec((nb, wc, KV_WIDTH), lambda i: (i, 0, 0)),
                  pl.BlockSpec((nb, wc, KV_WIDTH), lambda i: (i, 0, 0)),
                  pl.BlockSpec((t_new, LANES), full), pl.BlockSpec((t_new, LANES), full),
                  pl.BlockSpec((t_new, LANES), full),
                  pl.BlockSpec(memory_space=pltpu.SMEM)],
        out_specs=[pl.BlockSpec((tm, aw), blk), pl.BlockSpec((tm, KV_WIDTH), blk)],
        out_shape=[jax.ShapeDtypeStruct((rows, aw), BF16), jax.ShapeDtypeStruct((rows, KV_WIDTH), F32)],
        scratch_shapes=[pltpu.VMEM((tm, aw), F32), pltpu.VMEM((tm, aw), F32)],
        compiler_params=_params(32, ("arbitrary",)),
        name="attn_sample",
    )(q, kv, cache_k, cache_v, *tabs, sinks)


def _pool_kernel(u_ref, halo_ref, st_ref, pw_ref, scale_ref, o_ref, ext_ref, *, pos0):
    ti = pl.program_id(1)
    tg, tt, _ = u_ref.shape
    ext_ref[:, 0:POOL_HALO, :] = jnp.where(ti == 0, st_ref[...], halo_ref[...])
    ext_ref[:, POOL_HALO:, :] = u_ref[...]
    pos = pos0 + ti * tt + lax.broadcasted_iota(jnp.int32, (1, tt, 1), 1)
    for g, w in enumerate(POOL_WINDOWS):
        cols = slice(g * POOL_GROUP_WIDTH, (g + 1) * POOL_GROUP_WIDTH)
        acc = ext_ref[:, POOL_HALO:POOL_HALO + tt, cols]
        for k in range(1, w):
            acc = acc + ext_ref[:, POOL_HALO - k:POOL_HALO - k + tt, cols]
        inv_cnt = 1.0 / jnp.minimum(w, pos + 1).astype(F32)
        d = acc * inv_cnt - u_ref[:, :, cols]
        d = d.reshape(tg * tt, POOL_GROUP_WIDTH).astype(BF16)
        y = jnp.dot(d, pw_ref[g], preferred_element_type=F32) * scale_ref[:, cols]
        o_ref[:, cols] = y.astype(o_ref.dtype)


def pool_mix(u3, state16, pool_w, pool_scale, tg, tt, pos0):
    g, t, pw = u3.shape
    nt = t // tt
    hb = tt // POOL_HALO
    halo_src = u3 if nt > 1 else state16
    return pl.pallas_call(
        functools.partial(_pool_kernel, pos0=pos0),
        grid=(g // tg, nt),
        in_specs=[pl.BlockSpec((tg, tt, pw), lambda a, b: (a, b, 0)),
                  pl.BlockSpec((tg, POOL_HALO, pw), lambda a, b: (a, jnp.maximum(b * hb - 1, 0), 0)),
                  pl.BlockSpec((tg, POOL_HALO, pw), lambda a, b: (a, 0, 0)),
                  pl.BlockSpec(pool_w.shape, lambda a, b: (0, 0, 0)),
                  pl.BlockSpec((1, pw), lambda a, b: (0, 0))],
        out_specs=pl.BlockSpec((tg * tt, pw), lambda a, b: (a * nt + b, 0)),
        out_shape=jax.ShapeDtypeStruct((g * t, pw), BF16),
        scratch_shapes=[pltpu.VMEM((tg, POOL_HALO + tt, pw), F32)],
        compiler_params=_params(48, ("arbitrary", "arbitrary")),
        name="pool_mix",
    )(u3, halo_src, state16, pool_w, pool_scale)


def _layer_norm(z, g, b):
    mu = z.mean(-1, keepdims=True)
    zc = z - mu
    var = (zc * zc).mean(-1, keepdims=True)
    return zc * lax.rsqrt(var + LN_EPS) * g + b


def _ln1_kernel(x_ref, y_ref, g1_ref, sc_ref, sh_ref, lg_ref, lb_ref, x1_ref, h_ref, hb_ref, *, alpha):
    x = x_ref[...]
    z = alpha * x + g1_ref[...] * y_ref[...].reshape(x.shape)
    x1 = _layer_norm(z, lg_ref[...], lb_ref[...])
    h = x1 * (1.0 + sc_ref[...]) + sh_ref[...]
    x1_ref[...] = x1.reshape(x1_ref.shape)
    h2 = h.reshape(h_ref.shape)
    h_ref[...] = h2
    hb_ref[...] = h2.astype(BF16)


def ln1_modulate(x3, y, g1, sc, sh, ln_g, ln_b, tg, tt, alpha):
    g, t, d = x3.shape
    nt = t // tt
    rows = lambda a, b: (a * nt + b, 0)
    grp = lambda a, b: (a, 0, 0)
    vec = lambda a, b: (0, 0)
    return pl.pallas_call(
        functools.partial(_ln1_kernel, alpha=alpha),
        grid=(g // tg, nt),
        in_specs=[pl.BlockSpec((tg, tt, d), lambda a, b: (a, b, 0)), pl.BlockSpec((tg * tt, d), rows),
                  pl.BlockSpec((tg, 1, d), grp), pl.BlockSpec((tg, 1, d), grp), pl.BlockSpec((tg, 1, d), grp),
                  pl.BlockSpec((1, d), vec), pl.BlockSpec((1, d), vec)],
        out_specs=[pl.BlockSpec((tg * tt, d), rows)] * 3,
        out_shape=[jax.ShapeDtypeStruct((g * t, d), F32), jax.ShapeDtypeStruct((g * t, d), F32),
                   jax.ShapeDtypeStruct((g * t, d), BF16)],
        compiler_params=_params(48, ("arbitrary", "arbitrary")),
        name="ln1_modulate",
    )(x3, y, g1, sc, sh, ln_g, ln_b)


def _ln2_kernel(x_ref, y_ref, g2_ref, lg_ref, lb_ref, o_ref, *, alpha):
    z = alpha * x_ref[...] + g2_ref[...] * y_ref[...]
    o_ref[...] = _layer_norm(z, lg_ref[...], lb_ref[...])


def ln2(x3, y3, g2, ln_g, ln_b, tg, tt, alpha):
    g, t, d = x3.shape
    blk = pl.BlockSpec((tg, tt, d), lambda a, b: (a, b, 0))
    vec = pl.BlockSpec((1, d), lambda a, b: (0, 0))
    return pl.pallas_call(
        functools.partial(_ln2_kernel, alpha=alpha),
        grid=(g // tg, t // tt),
        in_specs=[blk, blk, pl.BlockSpec((tg, 1, d), lambda a, b: (a, 0, 0)), vec, vec],
        out_specs=blk,
        out_shape=jax.ShapeDtypeStruct((g, t, d), F32),
        compiler_params=_params(48, ("arbitrary", "arbitrary")),
        name="ln2",
    )(x3, y3, g2, ln_g, ln_b)


def _topk_rows(vals, k, payload=None):
    n = vals.shape[0]
    iota = lax.broadcasted_iota(jnp.int32, vals.shape, 0).astype(F32)
    out_v, out_i = [], []
    for _ in range(k):
        m = jnp.max(vals, axis=0, keepdims=True)
        am = jnp.min(jnp.where(vals == m, iota, float(n)), axis=0, keepdims=True)
        hit = iota == am
        out_v.append(m)
        out_i.append(am if payload is None else jnp.sum(jnp.where(hit, payload, 0.0), axis=0, keepdims=True))
        vals = jnp.where(hit, -jnp.inf, vals)
    return jnp.concatenate(out_v, axis=0), jnp.concatenate(out_i, axis=0)


def _peer_select_kernel(q_ref, sk_ref, idx_ref, gate_ref):
    tm = q_ref.shape[0]
    idx_rows, gate_rows = [], []
    for h in range(PEER_HEADS):
        tops = []
        for p in range(2):
            c0 = (h * 2 + p) * PEER_HALF
            qhp = q_ref[:, c0:c0 + PEER_HALF].astype(BF16)
            s_t = lax.dot_general(sk_ref[h, p], qhp, _NT, preferred_element_type=F32)
            tops.append(_topk_rows(s_t, PEER_TOPK))
        (v1, i1), (v2, i2) = tops
        cand = jnp.concatenate([v1[a:a + 1, :] + v2 for a in range(PEER_TOPK)], axis=0)
        cidx = jnp.concatenate([i1[a:a + 1, :] * float(N_KEYS) + i2 for a in range(PEER_TOPK)], axis=0)
        best, eidx = _topk_rows(cand, PEER_TOPK, payload=cidx)
        e = jnp.exp(best - best[0:1, :])
        gate_rows.append(e / e.sum(axis=0, keepdims=True))
        idx_rows.append(eidx)
    idx_ref[...] = jnp.concatenate(idx_rows, axis=0).T.astype(jnp.int32)
    gate_ref[...] = jnp.concatenate(gate_rows, axis=0).T


def peer_select(qp, subkeys_bf16, tm=128):
    rows, w = qp.shape
    tm = min(tm, rows)
    blk = lambda i: (i, 0)
    return pl.pallas_call(
        _peer_select_kernel,
        grid=(rows // tm,),
        in_specs=[pl.BlockSpec((tm, w), blk), pl.BlockSpec(subkeys_bf16.shape, lambda i: (0, 0, 0, 0))],
        out_specs=[pl.BlockSpec((tm, PEER_K), blk), pl.BlockSpec((tm, PEER_K), blk)],
        out_shape=[jax.ShapeDtypeStruct((rows, PEER_K), jnp.int32), jax.ShapeDtypeStruct((rows, PEER_K), F32)],
        compiler_params=_params(32, ("arbitrary",)),
        name="peer_select",
    )(qp, subkeys_bf16)


def _pack_kernel(u_ref, v_ref, o_ref):
    half = u_ref.shape[1] // 2

    def pack(ref):
        bits = lax.bitcast_convert_type(ref[...].astype(BF16).astype(F32), jnp.uint32)
        return (bits[:, :half] >> 16) | (bits[:, half:] & jnp.uint32(0xFFFF0000))

    o_ref[:, :half] = pack(u_ref)
    o_ref[:, half:] = pack(v_ref)


def pack_expert_tables(peer_u, peer_v, tr=256):
    e, d = peer_u.shape
    blk = pl.BlockSpec((tr, d), lambda i: (i, 0))
    return pl.pallas_call(
        _pack_kernel,
        grid=(e // tr,),
        in_specs=[blk, blk],
        out_specs=blk,
        out_shape=jax.ShapeDtypeStruct((e, d), jnp.uint32),
        compiler_params=_params(48, ("arbitrary",)),
        name="pack_tables",
    )(peer_u, peer_v)


def _unpack(w):
    lo = lax.bitcast_convert_type(w << 16, F32)
    hi = lax.bitcast_convert_type(w & jnp.uint32(0xFFFF0000), F32)
    return lo, hi


def _peer_mix_kernel(idx_ref, nidx_ref, x_ref, gate_ref, tab_ref, y_ref, buf_a, buf_b, sem_a, sem_b, *, tt, nsteps):
    i = pl.program_id(0)
    half = x_ref.shape[1] // 2
    nchunk = half // LANES

    def issue(ids, row, buf, sem, t):
        for k in range(PEER_K):
            pltpu.make_async_copy(tab_ref.at[pl.ds(ids[row, k], 1), :], buf.at[t, pl.ds(k, 1), :], sem.at[t]).start()

    def wait(buf, sem, t):
        pltpu.make_async_copy(tab_ref.at[pl.ds(0, PEER_K), :], buf.at[t], sem.at[t]).wait()

    @pl.when(i == 0)
    def _():
        def first(t, c):
            issue(idx_ref, t, buf_a, sem_a, t)
            return c
        lax.fori_loop(0, tt, first, 0)

    eye = (lax.broadcasted_iota(jnp.int32, (PEER_K, PEER_K), 0)
           == lax.broadcasted_iota(jnp.int32, (PEER_K, PEER_K), 1))
    row_id = lax.broadcasted_iota(jnp.int32, (tt, LANES), 0)
    y_ref[...] = jnp.zeros(y_ref.shape, F32)

    def mix(buf, t, row0):
        x = x_ref[pl.ds(row0 + t, 1), :]
        acc = jnp.zeros((PEER_K, LANES), F32)
        for ch in range(nchunk):
            cs = slice(ch * LANES, (ch + 1) * LANES)
            lo, hi = _unpack(buf[t, :, cs])
            acc = acc + lo * x[:, cs] + hi * x[:, half + ch * LANES:half + (ch + 1) * LANES]
        a = acc.sum(axis=1, keepdims=True)
        gate = jnp.sum(jnp.where(eye, gate_ref[pl.ds(row0 + t, 1), :], 0.0), axis=1, keepdims=True)
        act = 0.5 * a * (1.0 + lax.erf(a * (2.0 ** -0.5))) * gate
        act_b = jnp.broadcast_to(act, (PEER_K, LANES))
        mine = row_id == t
        rows = slice(row0, row0 + tt)
        for ch in range(nchunk):
            lo, hi = _unpack(buf[t, :, half + ch * LANES:half + (ch + 1) * LANES])
            for part, c0 in ((lo, ch * LANES), (hi, half + ch * LANES)):
                cs = slice(c0, c0 + LANES)
                y_ref[rows, cs] = jnp.where(mine, (act_b * part).sum(axis=0, keepdims=True), y_ref[rows, cs])

    def token_a(t, c):
        wait(buf_a, sem_a, t)
        issue(idx_ref, tt + t, buf_b, sem_b, t)
        mix(buf_a, t, 0)
        return c

    def token_b(t, c):
        wait(buf_b, sem_b, t)
        issue(nidx_ref, t, buf_a, sem_a, t)
        mix(buf_b, t, tt)
        return c

    lax.fori_loop(0, tt, token_a, 0)
    lax.fori_loop(0, tt, token_b, 0)

    @pl.when(i == nsteps - 1)
    def _():
        for t in range(tt):
            wait(buf_a, sem_a, t)


def peer_mix(h2, eidx, gate, table, tt=8):
    rows, d = h2.shape
    tb = 2 * tt
    nsteps = rows // tb
    blk = lambda i: (i, 0)
    nxt = lambda i: ((i + 1) % nsteps, 0)
    return pl.pallas_call(
        functools.partial(_peer_mix_kernel, tt=tt, nsteps=nsteps),
        grid=(nsteps,),
        in_specs=[pl.BlockSpec((tb, PEER_K), blk, memory_space=pltpu.SMEM),
                  pl.BlockSpec((tb, PEER_K), nxt, memory_space=pltpu.SMEM),
                  pl.BlockSpec((tb, d), blk), pl.BlockSpec((tb, PEER_K), blk),
                  pl.BlockSpec(memory_space=pl.ANY)],
        out_specs=pl.BlockSpec((tb, d), blk),
        out_shape=jax.ShapeDtypeStruct((rows, d), F32),
        scratch_shapes=[pltpu.VMEM((tt, PEER_K, d), jnp.uint32), pltpu.VMEM((tt, PEER_K, d), jnp.uint32),
                        pltpu.SemaphoreType.DMA((tt,)), pltpu.SemaphoreType.DMA((tt,))],
        compiler_params=_params(56, ("arbitrary",)),
        name="peer_mix",
    )(eidx, eidx, h2, gate, table)


def _layer(x3, mods, lw, pos0, tg, tt, prev=None):
    g, t, d = x3.shape
    sh1, sc1, g1, sh2, sc2, g2 = mods
    aw = N_HEADS * HEAD_DIM
    alpha = lw["alpha"]

    h = modulate(x3, sc1, sh1, tg, tt)
    q = matmul([(h, lw["w_q"])], name="in_q")
    kv = matmul([(h, lw["w_kv"])], name="in_kv")
    u = matmul([(h, lw["w_u"])], name="in_u")

    tabs = rope_tables(pos0 + jnp.arange(t, dtype=jnp.int32))
    if prev is None:
        attn, krot = attn_prompt(q, kv, tabs, lw["sinks"], g, t)
        state16 = jnp.zeros((g, POOL_HALO, u.shape[1]), F32)
        new_k = krot.reshape(g, t, N_KV_HEADS, HEAD_DIM)[:, -WINDOW:]
        new_v = kv[:, KV_WIDTH:].reshape(g, t, N_KV_HEADS, HEAD_DIM)[:, -WINDOW:]
    else:
        cache_k, cache_v, state = prev
        wc = cache_k.shape[1]
        attn, krot = attn_sample(q, kv, cache_k.reshape(g, wc, KV_WIDTH), cache_v.reshape(g, wc, KV_WIDTH),
                                 tabs, lw["sinks"], t)
        state16 = jnp.pad(state, ((0, 0), (POOL_HALO - POOL_STATE, 0), (0, 0)))
        new_k = jnp.concatenate([cache_k, krot.reshape(g, t, N_KV_HEADS, HEAD_DIM)], axis=1)[:, -wc:]
        new_v = jnp.concatenate([cache_v, kv[:, KV_WIDTH:].reshape(g, t, N_KV_HEADS, HEAD_DIM)], axis=1)[:, -wc:]
    u3 = u.reshape(g, t, u.shape[1])
    new_pool = jnp.concatenate([state16, u3], axis=1)[:, -POOL_STATE:]
    pooled = pool_mix(u3, state16, lw["pool_w"], lw["pool_scale"], min(tg, POOL_TG), tt, pos0)

    y1 = matmul([(attn, lw["w_out_a"]), (pooled, lw["w_out_p"])], name="out_proj")
    ln_tt = min(t, LN_ROWS)
    ln_tg = LN_ROWS // ln_tt
    x1, h2, h2b = ln1_modulate(x3, y1, g1, sc2, sh2, lw["ln1_g"], lw["ln1_b"], ln_tg, ln_tt, alpha)

    qp = matmul([(h2b, lw["peer_wq"])], name="peer_query")
    eidx, gate = peer_select(qp, lw["subkeys"])
    y2 = peer_mix(h2, eidx, gate, lw["table"])
    out = ln2(x1.reshape(g, t, d), y2.reshape(g, t, d), g2, lw["ln2_g"], lw["ln2_b"], ln_tg, ln_tt, alpha)
    return out, (new_k, new_v, new_pool)


def kernel(x_prompt, x_sample, cache_k, cache_v, state_pool, c_prompt, c_sample, w_ada, b_ada, w_in, sinks,
           pool_w, pool_scale, w_out, ln1_g, ln1_b, peer_wq, peer_subkeys, peer_u, peer_v, ln2_g, ln2_b):
    depth = w_ada.shape[0]
    bp, seq, d = x_prompt.shape
    bs, t_new, _ = x_sample.shape
    past_len = PAST_LEN
    alpha = (2 * depth) ** 0.25
    aw = N_HEADS * HEAD_DIM
    n_c = bp + bs
    pad = (-n_c) % 8

    xp, xs = x_prompt, x_sample
    outs = [[] for _ in range(6)]
    for l in range(depth):
        c_all = jnp.concatenate([c_prompt, c_sample, jnp.zeros((pad, d), F32)], axis=0)
        mod = adaln(c_all, w_ada[l], b_ada[l][None, :])
        mod_p = mod[:bp].reshape(bp, N_MOD, 1, d)
        mod_s = mod[bp:n_c].reshape(bs, N_MOD, 1, d)
        mods_p = tuple(mod_p[:, i] for i in range(N_MOD))
        mods_s = tuple(mod_s[:, i] for i in range(N_MOD))
        w_in_b = w_in[l].astype(BF16)
        w_out_b = w_out[l].astype(BF16)
        lw = dict(
            alpha=alpha,
            w_q=w_in_b[:, :aw], w_kv=w_in_b[:, aw:aw + 2 * KV_WIDTH], w_u=w_in_b[:, aw + 2 * KV_WIDTH:],
            sinks=sinks[l], pool_w=pool_w[l].astype(BF16), pool_scale=pool_scale[l][None, :],
            w_out_a=w_out_b[:aw], w_out_p=w_out_b[aw:],
            ln1_g=ln1_g[l][None, :], ln1_b=ln1_b[l][None, :], ln2_g=ln2_g[l][None, :], ln2_b=ln2_b[l][None, :],
            peer_wq=peer_wq[l].astype(BF16), subkeys=peer_subkeys[l].astype(BF16),
            table=pack_expert_tables(peer_u[l], peer_v[l]),
        )
        xp, (k1, v1, p1) = _layer(xp, mods_p, lw, 0, 1, 512)
        xs, (k2, v2, p2) = _layer(xs, mods_s, lw, past_len, 64, t_new, prev=(cache_k[l], cache_v[l], state_pool[l]))
        for lst, val in zip(outs, (k1, v1, p1, k2, v2, p2)):
            lst.append(val)
    return (xp, xs) + tuple(jnp.stack(o) for o in outs)
```

```python
import functools

import jax
import jax.numpy as jnp
from jax import lax
from jax.experimental import pallas as pl
from jax.experimental.pallas import tpu as pltpu

F32 = jnp.float32
BF16 = jnp.bfloat16

LANES = 128
HEAD_DIM = 64
N_HEADS = 32
N_KV_HEADS = 4
GROUP = N_HEADS // N_KV_HEADS
KV_WIDTH = N_KV_HEADS * HEAD_DIM
WINDOW = 128
ROT_DIM = HEAD_DIM // 4
ROT_HALF = ROT_DIM // 2
ROPE_THETA = 500000.0
POOL_WINDOWS = (2, 4, 8, 16)
POOL_GROUP_WIDTH = 512
POOL_HALO = 16
POOL_STATE = 15
PEER_HEADS = 8
PEER_HALF = 128
N_KEYS = 128
PEER_TOPK = 16
PEER_K = PEER_HEADS * PEER_TOPK
N_MOD = 6
PAST_LEN = 8192
LN_ROWS = 128
POOL_TG = 16
LN_EPS = 1e-5
NEG_INF = -1e30


def _params(vmem_mb, sem=None):
    return pltpu.CompilerParams(dimension_semantics=sem, vmem_limit_bytes=vmem_mb << 20)


def _ada_kernel(c_ref, w_ref, b_ref, o_ref):
    c = c_ref[...]
    s = (c * jax.nn.sigmoid(c)).astype(BF16)
    o_ref[...] = jnp.dot(s, w_ref[...].astype(BF16), preferred_element_type=F32) + b_ref[...]


def adaln(c_all, w_ada, b_ada, tn=512):
    r, d = c_all.shape
    n = w_ada.shape[1]
    return pl.pallas_call(
        _ada_kernel,
        grid=(n // tn,),
        in_specs=[pl.BlockSpec((r, d), lambda j: (0, 0)),
                  pl.BlockSpec((d, tn), lambda j: (0, j)),
                  pl.BlockSpec((1, tn), lambda j: (0, j))],
        out_specs=pl.BlockSpec((r, tn), lambda j: (0, j)),
        out_shape=jax.ShapeDtypeStruct((r, n), F32),
        compiler_params=_params(48, ("arbitrary",)),
        name="adaln",
    )(c_all, w_ada, b_ada)


def _mod_kernel(x_ref, sc_ref, sh_ref, o_ref):
    h = x_ref[...] * (1.0 + sc_ref[...]) + sh_ref[...]
    o_ref[...] = h.reshape(o_ref.shape).astype(o_ref.dtype)


def modulate(x3, sc, sh, tg, tt):
    g, t, d = x3.shape
    nt = t // tt
    return pl.pallas_call(
        _mod_kernel,
        grid=(g // tg, nt),
        in_specs=[pl.BlockSpec((tg, tt, d), lambda a, b: (a, b, 0)),
                  pl.BlockSpec((tg, 1, d), lambda a, b: (a, 0, 0)),
                  pl.BlockSpec((tg, 1, d), lambda a, b: (a, 0, 0))],
        out_specs=pl.BlockSpec((tg * tt, d), lambda a, b: (a * nt + b, 0)),
        out_shape=jax.ShapeDtypeStruct((g * t, d), BF16),
        compiler_params=_params(48, ("arbitrary", "arbitrary")),
        name="modulate",
    )(x3, sc, sh)


def _mm_kernel(*refs, n_pairs):
    o_ref = refs[-1]
    acc = None
    for p in range(n_pairs):
        part = jnp.dot(refs[2 * p][...], refs[2 * p + 1][...], preferred_element_type=F32)
        acc = part if acc is None else acc + part
    o_ref[...] = acc.astype(o_ref.dtype)


def matmul(pairs, tm=1024, tn=512, out_dtype=F32, name="matmul"):
    m = pairs[0][0].shape[0]
    n = pairs[0][1].shape[1]
    tm, tn = min(tm, m), min(tn, n)
    in_specs, args = [], []
    for x, w in pairs:
        k = x.shape[1]
        in_specs += [pl.BlockSpec((tm, k), lambda i, j: (i, 0)), pl.BlockSpec((k, tn), lambda i, j: (0, j))]
        args += [x, w]
    return pl.pallas_call(
        functools.partial(_mm_kernel, n_pairs=len(pairs)),
        grid=(m // tm, n // tn),
        in_specs=in_specs,
        out_specs=pl.BlockSpec((tm, tn), lambda i, j: (i, j)),
        out_shape=jax.ShapeDtypeStruct((m, n), out_dtype),
        compiler_params=_params(48, ("arbitrary", "arbitrary")),
        name=name,
    )(*args)


def rope_tables(pos):
    inv_freq = ROPE_THETA ** (-jnp.arange(ROT_HALF, dtype=F32) / ROT_HALF)
    ang = pos.astype(F32)[:, None] * inv_freq[None, :]
    cos, sin = jnp.cos(ang), jnp.sin(ang)
    t = pos.shape[0]
    ones = jnp.ones((t, HEAD_DIM - ROT_DIM), F32)
    zeros_h = jnp.zeros((t, ROT_HALF), F32)
    zeros_r = jnp.zeros((t, HEAD_DIM - ROT_DIM), F32)
    c = jnp.concatenate([cos, cos, ones], axis=1)
    s1 = jnp.concatenate([zeros_h, sin, zeros_r], axis=1)
    s2 = jnp.concatenate([-sin, zeros_h, zeros_r], axis=1)
    rep = LANES // HEAD_DIM
    return tuple(jnp.tile(a, (1, rep)) for a in (c, s1, s2))


def _rope(x, c, s1, s2):
    rows, w = x.shape
    reps = (rows // c.shape[0], w // LANES)
    ct, s1t, s2t = (jnp.tile(a, reps) for a in (c, s1, s2))
    return x * ct + pltpu.roll(x, ROT_HALF, 1) * s1t + pltpu.roll(x, w - ROT_HALF, 1) * s2t


def _sink_softmax_pv(pieces, sink):
    m = sink
    for s, _ in pieces:
        m = jnp.maximum(m, s.max(-1, keepdims=True))
    denom = jnp.exp(sink - m)
    o = None
    for s, v in pieces:
        p = jnp.exp(s - m)
        denom = denom + p.sum(-1, keepdims=True)
        pv = jnp.dot(p.astype(BF16), v, preferred_element_type=F32)
        o = pv if o is None else o + pv
    return o / denom


_NT = (((1,), (1,)), ((), ()))


def _attn_prompt_kernel(q_ref, kp_ref, kc_ref, vp_ref, vc_ref, cc_ref, s1c_ref, s2c_ref,
                        cp_ref, s1p_ref, s2p_ref, sink_ref, o_ref, krot_ref):
    i = pl.program_id(1)
    q = (_rope(q_ref[...], cc_ref[...], s1c_ref[...], s2c_ref[...]) * (HEAD_DIM ** -0.5)).astype(BF16)
    kc = _rope(kc_ref[...], cc_ref[...], s1c_ref[...], s2c_ref[...])
    krot_ref[...] = kc
    kp = _rope(kp_ref[...], cp_ref[...], s1p_ref[...], s2p_ref[...])
    kc, kp = kc.astype(BF16), kp.astype(BF16)
    vc, vp = vc_ref[...].astype(BF16), vp_ref[...].astype(BF16)
    gr = GROUP * WINDOW
    r = lax.broadcasted_iota(jnp.int32, (gr, WINDOW), 0) % WINDOW
    j = lax.broadcasted_iota(jnp.int32, (gr, WINDOW), 1)
    mask_c = j <= r
    mask_p = (j >= r) & (i > 0)
    outs = []
    for h in range(N_KV_HEADS):
        hs = slice(h * HEAD_DIM, (h + 1) * HEAD_DIM)
        qg = jnp.concatenate([q[:, (h * GROUP + g) * HEAD_DIM:(h * GROUP + g + 1) * HEAD_DIM]
                              for g in range(GROUP)], axis=0)
        sink = jnp.concatenate([jnp.full((WINDOW, 1), sink_ref[h * GROUP + g], F32) for g in range(GROUP)], axis=0)
        sp = jnp.where(mask_p, lax.dot_general(qg, kp[:, hs], _NT, preferred_element_type=F32), NEG_INF)
        sc = jnp.where(mask_c, lax.dot_general(qg, kc[:, hs], _NT, preferred_element_type=F32), NEG_INF)
        o = _sink_softmax_pv([(sp, vp[:, hs]), (sc, vc[:, hs])], sink)
        outs += [o[g * WINDOW:(g + 1) * WINDOW, :] for g in range(GROUP)]
    o_ref[...] = jnp.concatenate(outs, axis=1).astype(o_ref.dtype)


def attn_prompt(q, kv, tabs, sinks, batch, seq):
    nb = seq // WINDOW
    aw = N_HEADS * HEAD_DIM
    cur = lambda b, i: (b * nb + i, 0)
    prev = lambda b, i: (b * nb + jnp.maximum(i - 1, 0), 0)
    cur_v = lambda b, i: (b * nb + i, 1)
    prev_v = lambda b, i: (b * nb + jnp.maximum(i - 1, 0), 1)
    tcur = lambda b, i: (i, 0)
    tprev = lambda b, i: (jnp.maximum(i - 1, 0), 0)
    tspec = lambda f: pl.BlockSpec((WINDOW, LANES), f)
    return pl.pallas_call(
        _attn_prompt_kernel,
        grid=(batch, nb),
        in_specs=[pl.BlockSpec((WINDOW, aw), cur),
                  pl.BlockSpec((WINDOW, KV_WIDTH), prev), pl.BlockSpec((WINDOW, KV_WIDTH), cur),
                  pl.BlockSpec((WINDOW, KV_WIDTH), prev_v), pl.BlockSpec((WINDOW, KV_WIDTH), cur_v),
                  tspec(tcur), tspec(tcur), tspec(tcur), tspec(tprev), tspec(tprev), tspec(tprev),
                  pl.BlockSpec(memory_space=pltpu.SMEM)],
        out_specs=[pl.BlockSpec((WINDOW, aw), cur), pl.BlockSpec((WINDOW, KV_WIDTH), cur)],
        out_shape=[jax.ShapeDtypeStruct((batch * seq, aw), BF16),
                   jax.ShapeDtypeStruct((batch * seq, KV_WIDTH), F32)],
        compiler_params=_params(32, ("arbitrary", "arbitrary")),
        name="attn_prompt",
    )(q, kv, kv, kv, kv, *tabs, *tabs, sinks)


def _attn_sample_kernel(q_ref, kv_ref, ck_ref, cv_ref, c_ref, s1_ref, s2_ref, sink_ref, o_ref, krot_ref,
                        qs_ref, os_ref, *, nb, t_new):
    tabs = (c_ref[...], s1_ref[...], s2_ref[...])
    qs_ref[...] = (_rope(q_ref[...], *tabs) * (HEAD_DIM ** -0.5)).astype(BF16).astype(F32)
    krot_ref[...] = _rope(kv_ref[:, :KV_WIDTH], *tabs)
    wc = ck_ref.shape[1]
    gr = GROUP * t_new
    tok = lax.broadcasted_iota(jnp.int32, (gr, wc), 0) % t_new
    diff_c = tok + wc - lax.broadcasted_iota(jnp.int32, (gr, wc), 1)
    mask_c = (diff_c >= 0) & (diff_c <= WINDOW)
    mask_n = (lax.broadcasted_iota(jnp.int32, (gr, t_new), 1)
              <= lax.broadcasted_iota(jnp.int32, (gr, t_new), 0) % t_new)
    sink_cols = [jnp.concatenate([jnp.full((t_new, 1), sink_ref[h * GROUP + g], F32) for g in range(GROUP)], axis=0)
                 for h in range(N_KV_HEADS)]

    def body(n, carry):
        rows = pl.ds(pl.multiple_of(n * t_new, t_new), t_new)
        q = qs_ref[rows, :]
        kn = krot_ref[rows, :].astype(BF16)
        vn = kv_ref[rows, KV_WIDTH:].astype(BF16)
        kc = ck_ref[n].astype(BF16)
        vc = cv_ref[n].astype(BF16)
        outs = []
        for h in range(N_KV_HEADS):
            hs = slice(h * HEAD_DIM, (h + 1) * HEAD_DIM)
            qg = jnp.concatenate([q[:, (h * GROUP + g) * HEAD_DIM:(h * GROUP + g + 1) * HEAD_DIM]
                                  for g in range(GROUP)], axis=0).astype(BF16)
            sc = jnp.where(mask_c, lax.dot_general(qg, kc[:, hs], _NT, preferred_element_type=F32), NEG_INF)
            sn = jnp.where(mask_n, lax.dot_general(qg, kn[:, hs], _NT, preferred_element_type=F32), NEG_INF)
            o = _sink_softmax_pv([(sc, vc[:, hs]), (sn, vn[:, hs])], sink_cols[h])
            outs += [o[g * t_new:(g + 1) * t_new, :] for g in range(GROUP)]
        os_ref[rows, :] = jnp.concatenate(outs, axis=1)
        return carry

    lax.fori_loop(0, nb, body, 0)
    o_ref[...] = os_ref[...].astype(o_ref.dtype)


def attn_sample(q, kv, cache_k, cache_v, tabs, sinks, t_new, nb=8):
    rows = q.shape[0]
    n = rows // t_new
    wc = cache_k.shape[1]
    aw = N_HEADS * HEAD_DIM
    tm = nb * t_new
    blk = lambda i: (i, 0)
    full = lambda i: (0, 0)
    return pl.pallas_call(
        functools.partial(_attn_sample_kernel, nb=nb, t_new=t_new),
        grid=(n // nb,),
        in_specs=[pl.BlockSpec((tm, aw), blk), pl.BlockSpec((tm, 2 * KV_WIDTH), blk),
                  pl.BlockSpec((nb, wc, KV_WIDTH), lambda i: (i, 0, 0)),
                  pl.BlockSpec((nb, wc, KV_WIDTH), lambda i: (i, 0, 0)),
                  pl.BlockSpec((t_new, LANES), full), pl.BlockSpec((t_new, LANES), full),
                  pl.BlockSpec((t_new, LANES), full),
                  pl.BlockSpec(memory_space=pltpu.SMEM)],
        out_specs=[pl.BlockSpec((tm, aw), blk), pl.BlockSpec((tm, KV_WIDTH), blk)],
        out_shape=[jax.ShapeDtypeStruct((rows, aw), BF16), jax.ShapeDtypeStruct((rows, KV_WIDTH), F32)],
        scratch_shapes=[pltpu.VMEM((tm, aw), F32), pltpu.VMEM((tm, aw), F32)],
        compiler_params=_params(32, ("arbitrary",)),
        name="attn_sample",
    )(q, kv, cache_k, cache_v, *tabs, sinks)


def _pool_kernel(u_ref, halo_ref, st_ref, pw_ref, scale_ref, o_ref, ext_ref, *, pos0):
    ti = pl.program_id(1)
    tg, tt, _ = u_ref.shape
    ext_ref[:, 0:POOL_HALO, :] = jnp.where(ti == 0, st_ref[...], halo_ref[...])
    ext_ref[:, POOL_HALO:, :] = u_ref[...]
    pos = pos0 + ti * tt + lax.broadcasted_iota(jnp.int32, (1, tt, 1), 1)
    for g, w in enumerate(POOL_WINDOWS):
        cols = slice(g * POOL_GROUP_WIDTH, (g + 1) * POOL_GROUP_WIDTH)
        acc = ext_ref[:, POOL_HALO:POOL_HALO + tt, cols]
        for k in range(1, w):
            acc = acc + ext_ref[:, POOL_HALO - k:POOL_HALO - k + tt, cols]
        inv_cnt = 1.0 / jnp.minimum(w, pos + 1).astype(F32)
        d = acc * inv_cnt - u_ref[:, :, cols]
        d = d.reshape(tg * tt, POOL_GROUP_WIDTH).astype(BF16)
        y = jnp.dot(d, pw_ref[g], preferred_element_type=F32) * scale_ref[:, cols]
        o_ref[:, cols] = y.astype(o_ref.dtype)


def pool_mix(u3, state16, pool_w, pool_scale, tg, tt, pos0):
    g, t, pw = u3.shape
    nt = t // tt
    hb = tt // POOL_HALO
    halo_src = u3 if nt > 1 else state16
    return pl.pallas_call(
        functools.partial(_pool_kernel, pos0=pos0),
        grid=(g // tg, nt),
        in_specs=[pl.BlockSpec((tg, tt, pw), lambda a, b: (a, b, 0)),
                  pl.BlockSpec((tg, POOL_HALO, pw), lambda a, b: (a, jnp.maximum(b * hb - 1, 0), 0)),
                  pl.BlockSpec((tg, POOL_HALO, pw), lambda a, b: (a, 0, 0)),
                  pl.BlockSpec(pool_w.shape, lambda a, b: (0, 0, 0)),
                  pl.BlockSpec((1, pw), lambda a, b: (0, 0))],
        out_specs=pl.BlockSpec((tg * tt, pw), lambda a, b: (a * nt + b, 0)),
        out_shape=jax.ShapeDtypeStruct((g * t, pw), BF16),
        scratch_shapes=[pltpu.VMEM((tg, POOL_HALO + tt, pw), F32)],
        compiler_params=_params(48, ("arbitrary", "arbitrary")),
        name="pool_mix",
    )(u3, halo_src, state16, pool_w, pool_scale)


def _layer_norm(z, g, b):
    mu = z.mean(-1, keepdims=True)
    zc = z - mu
    var = (zc * zc).mean(-1, keepdims=True)
    return zc * lax.rsqrt(var + LN_EPS) * g + b


def _ln1_kernel(x_ref, y_ref, g1_ref, sc_ref, sh_ref, lg_ref, lb_ref, x1_ref, h_ref, hb_ref, *, alpha):
    x = x_ref[...]
    z = alpha * x + g1_ref[...] * y_ref[...].reshape(x.shape)
    x1 = _layer_norm(z, lg_ref[...], lb_ref[...])
    h = x1 * (1.0 + sc_ref[...]) + sh_ref[...]
    x1_ref[...] = x1.reshape(x1_ref.shape)
    h2 = h.reshape(h_ref.shape)
    h_ref[...] = h2
    hb_ref[...] = h2.astype(BF16)


def ln1_modulate(x3, y, g1, sc, sh, ln_g, ln_b, tg, tt, alpha):
    g, t, d = x3.shape
    nt = t // tt
    rows = lambda a, b: (a * nt + b, 0)
    grp = lambda a, b: (a, 0, 0)
    vec = lambda a, b: (0, 0)
    return pl.pallas_call(
        functools.partial(_ln1_kernel, alpha=alpha),
        grid=(g // tg, nt),
        in_specs=[pl.BlockSpec((tg, tt, d), lambda a, b: (a, b, 0)), pl.BlockSpec((tg * tt, d), rows),
                  pl.BlockSpec((tg, 1, d), grp), pl.BlockSpec((tg, 1, d), grp), pl.BlockSpec((tg, 1, d), grp),
                  pl.BlockSpec((1, d), vec), pl.BlockSpec((1, d), vec)],
        out_specs=[pl.BlockSpec((tg * tt, d), rows)] * 3,
        out_shape=[jax.ShapeDtypeStruct((g * t, d), F32), jax.ShapeDtypeStruct((g * t, d), F32),
                   jax.ShapeDtypeStruct((g * t, d), BF16)],
        compiler_params=_params(48, ("arbitrary", "arbitrary")),
        name="ln1_modulate",
    )(x3, y, g1, sc, sh, ln_g, ln_b)


def _ln2_kernel(x_ref, y_ref, g2_ref, lg_ref, lb_ref, o_ref, *, alpha):
    z = alpha * x_ref[...] + g2_ref[...] * y_ref[...]
    o_ref[...] = _layer_norm(z, lg_ref[...], lb_ref[...])


def ln2(x3, y3, g2, ln_g, ln_b, tg, tt, alpha):
    g, t, d = x3.shape
    blk = pl.BlockSpec((tg, tt, d), lambda a, b: (a, b, 0))
    vec = pl.BlockSpec((1, d), lambda a, b: (0, 0))
    return pl.pallas_call(
        functools.partial(_ln2_kernel, alpha=alpha),
        grid=(g // tg, t // tt),
        in_specs=[blk, blk, pl.BlockSpec((tg, 1, d), lambda a, b: (a, 0, 0)), vec, vec],
        out_specs=blk,
        out_shape=jax.ShapeDtypeStruct((g, t, d), F32),
        compiler_params=_params(48, ("arbitrary", "arbitrary")),
        name="ln2",
    )(x3, y3, g2, ln_g, ln_b)


def _topk_rows(vals, k, payload=None):
    n = vals.shape[0]
    iota = lax.broadcasted_iota(jnp.int32, vals.shape, 0).astype(F32)
    out_v, out_i = [], []
    for _ in range(k):
        m = jnp.max(vals, axis=0, keepdims=True)
        am = jnp.min(jnp.where(vals == m, iota, float(n)), axis=0, keepdims=True)
        hit = iota == am
        out_v.append(m)
        out_i.append(am if payload is None else jnp.sum(jnp.where(hit, payload, 0.0), axis=0, keepdims=True))
        vals = jnp.where(hit, -jnp.inf, vals)
    return jnp.concatenate(out_v, axis=0), jnp.concatenate(out_i, axis=0)


def _peer_select_kernel(q_ref, sk_ref, idx_ref, gate_ref):
    tm = q_ref.shape[0]
    idx_rows, gate_rows = [], []
    for h in range(PEER_HEADS):
        tops = []
        for p in range(2):
            c0 = (h * 2 + p) * PEER_HALF
            qhp = q_ref[:, c0:c0 + PEER_HALF].astype(BF16)
            s_t = lax.dot_general(sk_ref[h, p], qhp, _NT, preferred_element_type=F32)
            tops.append(_topk_rows(s_t, PEER_TOPK))
        (v1, i1), (v2, i2) = tops
        nb = [PEER_TOPK // (a + 1) for a in range(PEER_TOPK)]
        pad = (-sum(nb)) % 8
        cand = jnp.concatenate([v1[a:a + 1, :] + v2[:nb[a], :] for a in range(PEER_TOPK)]
                               + [jnp.full((pad, tm), -jnp.inf, F32)], axis=0)
        cidx = jnp.concatenate([i1[a:a + 1, :] * float(N_KEYS) + i2[:nb[a], :] for a in range(PEER_TOPK)]
                               + [jnp.zeros((pad, tm), F32)], axis=0)
        best, eidx = _topk_rows(cand, PEER_TOPK, payload=cidx)
        e = jnp.exp(best - best[0:1, :])
        gate_rows.append(e / e.sum(axis=0, keepdims=True))
        idx_rows.append(eidx)
    idx_ref[...] = jnp.concatenate(idx_rows, axis=0).T.astype(jnp.int32)
    gate_ref[...] = jnp.concatenate(gate_rows, axis=0).T


def peer_select(qp, subkeys_bf16, tm=128):
    rows, w = qp.shape
    tm = min(tm, rows)
    blk = lambda i: (i, 0)
    return pl.pallas_call(
        _peer_select_kernel,
        grid=(rows // tm,),
        in_specs=[pl.BlockSpec((tm, w), blk), pl.BlockSpec(subkeys_bf16.shape, lambda i: (0, 0, 0, 0))],
        out_specs=[pl.BlockSpec((tm, PEER_K), blk), pl.BlockSpec((tm, PEER_K), blk)],
        out_shape=[jax.ShapeDtypeStruct((rows, PEER_K), jnp.int32), jax.ShapeDtypeStruct((rows, PEER_K), F32)],
        compiler_params=_params(32, ("arbitrary",)),
        name="peer_select",
    )(qp, subkeys_bf16)


def _pack_kernel(u_ref, v_ref, o_ref):
    half = u_ref.shape[1] // 2

    def pack(ref):
        bits = lax.bitcast_convert_type(ref[...].astype(BF16).astype(F32), jnp.uint32)
        return (bits[:, :half] >> 16) | (bits[:, half:] & jnp.uint32(0xFFFF0000))

    o_ref[:, :half] = pack(u_ref)
    o_ref[:, half:] = pack(v_ref)


def pack_expert_tables(peer_u, peer_v, tr=256):
    e, d = peer_u.shape
    blk = pl.BlockSpec((tr, d), lambda i: (i, 0))
    return pl.pallas_call(
        _pack_kernel,
        grid=(e // tr,),
        in_specs=[blk, blk],
        out_specs=blk,
        out_shape=jax.ShapeDtypeStruct((e, d), jnp.uint32),
        compiler_params=_params(48, ("arbitrary",)),
        name="pack_tables",
    )(peer_u, peer_v)


def _unpack(w):
    lo = lax.bitcast_convert_type(w << 16, F32)
    hi = lax.bitcast_convert_type(w & jnp.uint32(0xFFFF0000), F32)
    return lo, hi


def _peer_mix_kernel(idx_ref, nidx_ref, x_ref, gate_ref, tab_ref, y_ref, buf_a, buf_b, sem_a, sem_b, *, tt, nsteps):
    i = pl.program_id(0)
    half = x_ref.shape[1] // 2
    nchunk = half // LANES

    def issue(ids, row, buf, sem, t):
        for k in range(PEER_K):
            pltpu.make_async_copy(tab_ref.at[pl.ds(ids[row, k], 1), :], buf.at[t, pl.ds(k, 1), :], sem.at[t]).start()

    def wait(buf, sem, t):
        pltpu.make_async_copy(tab_ref.at[pl.ds(0, PEER_K), :], buf.at[t], sem.at[t]).wait()

    @pl.when(i == 0)
    def _():
        def first(t, c):
            issue(idx_ref, t, buf_a, sem_a, t)
            return c
        lax.fori_loop(0, tt, first, 0)

    eye = (lax.broadcasted_iota(jnp.int32, (PEER_K, PEER_K), 0)
           == lax.broadcasted_iota(jnp.int32, (PEER_K, PEER_K), 1))
    row_id = lax.broadcasted_iota(jnp.int32, (tt, LANES), 0)
    y_ref[...] = jnp.zeros(y_ref.shape, F32)

    def mix(buf, t, row0):
        x = x_ref[pl.ds(row0 + t, 1), :]
        acc = jnp.zeros((PEER_K, LANES), F32)
        for ch in range(nchunk):
            cs = slice(ch * LANES, (ch + 1) * LANES)
            lo, hi = _unpack(buf[t, :, cs])
            acc = acc + lo * x[:, cs] + hi * x[:, half + ch * LANES:half + (ch + 1) * LANES]
        a = acc.sum(axis=1, keepdims=True)
        gate = jnp.sum(jnp.where(eye, gate_ref[pl.ds(row0 + t, 1), :], 0.0), axis=1, keepdims=True)
        act = 0.5 * a * (1.0 + lax.erf(a * (2.0 ** -0.5))) * gate
        act_b = jnp.broadcast_to(act, (PEER_K, LANES))
        mine = row_id == t
        rows = slice(row0, row0 + tt)
        for ch in range(nchunk):
            lo, hi = _unpack(buf[t, :, half + ch * LANES:half + (ch + 1) * LANES])
            for part, c0 in ((lo, ch * LANES), (hi, half + ch * LANES)):
                cs = slice(c0, c0 + LANES)
                y_ref[rows, cs] = jnp.where(mine, (act_b * part).sum(axis=0, keepdims=True), y_ref[rows, cs])

    def token_a(t, c):
        wait(buf_a, sem_a, t)
        issue(idx_ref, tt + t, buf_b, sem_b, t)
        mix(buf_a, t, 0)
        return c

    def token_b(t, c):
        wait(buf_b, sem_b, t)
        issue(nidx_ref, t, buf_a, sem_a, t)
        mix(buf_b, t, tt)
        return c

    lax.fori_loop(0, tt, token_a, 0)
    lax.fori_loop(0, tt, token_b, 0)

    @pl.when(i == nsteps - 1)
    def _():
        for t in range(tt):
            wait(buf_a, sem_a, t)


def peer_mix(h2, eidx, gate, table, tt=8):
    rows, d = h2.shape
    tb = 2 * tt
    nsteps = rows // tb
    blk = lambda i: (i, 0)
    nxt = lambda i: ((i + 1) % nsteps, 0)
    return pl.pallas_call(
        functools.partial(_peer_mix_kernel, tt=tt, nsteps=nsteps),
        grid=(nsteps,),
        in_specs=[pl.BlockSpec((tb, PEER_K), blk, memory_space=pltpu.SMEM),
                  pl.BlockSpec((tb, PEER_K), nxt, memory_space=pltpu.SMEM),
                  pl.BlockSpec((tb, d), blk), pl.BlockSpec((tb, PEER_K), blk),
                  pl.BlockSpec(memory_space=pl.ANY)],
        out_specs=pl.BlockSpec((tb, d), blk),
        out_shape=jax.ShapeDtypeStruct((rows, d), F32),
        scratch_shapes=[pltpu.VMEM((tt, PEER_K, d), jnp.uint32), pltpu.VMEM((tt, PEER_K, d), jnp.uint32),
                        pltpu.SemaphoreType.DMA((tt,)), pltpu.SemaphoreType.DMA((tt,))],
        compiler_params=_params(56, ("arbitrary",)),
        name="peer_mix",
    )(eidx, eidx, h2, gate, table)


def _layer(x3, mods, lw, pos0, tg, tt, prev=None):
    g, t, d = x3.shape
    sh1, sc1, g1, sh2, sc2, g2 = mods
    aw = N_HEADS * HEAD_DIM
    alpha = lw["alpha"]

    h = modulate(x3, sc1, sh1, tg, tt)
    q = matmul([(h, lw["w_q"])], name="in_q")
    kv = matmul([(h, lw["w_kv"])], name="in_kv")
    u = matmul([(h, lw["w_u"])], name="in_u")

    tabs = rope_tables(pos0 + jnp.arange(t, dtype=jnp.int32))
    if prev is None:
        attn, krot = attn_prompt(q, kv, tabs, lw["sinks"], g, t)
        state16 = jnp.zeros((g, POOL_HALO, u.shape[1]), F32)
        new_k = krot.reshape(g, t, N_KV_HEADS, HEAD_DIM)[:, -WINDOW:]
        new_v = kv[:, KV_WIDTH:].reshape(g, t, N_KV_HEADS, HEAD_DIM)[:, -WINDOW:]
    else:
        cache_k, cache_v, state = prev
        wc = cache_k.shape[1]
        attn, krot = attn_sample(q, kv, cache_k.reshape(g, wc, KV_WIDTH), cache_v.reshape(g, wc, KV_WIDTH),
                                 tabs, lw["sinks"], t)
        state16 = jnp.pad(state, ((0, 0), (POOL_HALO - POOL_STATE, 0), (0, 0)))
        new_k = jnp.concatenate([cache_k, krot.reshape(g, t, N_KV_HEADS, HEAD_DIM)], axis=1)[:, -wc:]
        new_v = jnp.concatenate([cache_v, kv[:, KV_WIDTH:].reshape(g, t, N_KV_HEADS, HEAD_DIM)], axis=1)[:, -wc:]
    u3 = u.reshape(g, t, u.shape[1])
    new_pool = jnp.concatenate([state16, u3], axis=1)[:, -POOL_STATE:]
    pooled = pool_mix(u3, state16, lw["pool_w"], lw["pool_scale"], min(tg, POOL_TG), tt, pos0)

    y1 = matmul([(attn, lw["w_out_a"]), (pooled, lw["w_out_p"])], name="out_proj")
    ln_tt = min(t, LN_ROWS)
    ln_tg = LN_ROWS // ln_tt
    x1, h2, h2b = ln1_modulate(x3, y1, g1, sc2, sh2, lw["ln1_g"], lw["ln1_b"], ln_tg, ln_tt, alpha)

    qp = matmul([(h2b, lw["peer_wq"])], name="peer_query")
    eidx, gate = peer_select(qp, lw["subkeys"])
    y2 = peer_mix(h2, eidx, gate, lw["table"])
    out = ln2(x1.reshape(g, t, d), y2.reshape(g, t, d), g2, lw["ln2_g"], lw["ln2_b"], ln_tg, ln_tt, alpha)
    return out, (new_k, new_v, new_pool)


def kernel(x_prompt, x_sample, cache_k, cache_v, state_pool, c_prompt, c_sample, w_ada, b_ada, w_in, sinks,
           pool_w, pool_scale, w_out, ln1_g, ln1_b, peer_wq, peer_subkeys, peer_u, peer_v, ln2_g, ln2_b):
    depth = w_ada.shape[0]
    bp, seq, d = x_prompt.shape
    bs, t_new, _ = x_sample.shape
    past_len = PAST_LEN
    alpha = (2 * depth) ** 0.25
    aw = N_HEADS * HEAD_DIM
    n_c = bp + bs
    pad = (-n_c) % 8

    xp, xs = x_prompt, x_sample
    outs = [[] for _ in range(6)]
    for l in range(depth):
        c_all = jnp.concatenate([c_prompt, c_sample, jnp.zeros((pad, d), F32)], axis=0)
        mod = adaln(c_all, w_ada[l], b_ada[l][None, :])
        mod_p = mod[:bp].reshape(bp, N_MOD, 1, d)
        mod_s = mod[bp:n_c].reshape(bs, N_MOD, 1, d)
        mods_p = tuple(mod_p[:, i] for i in range(N_MOD))
        mods_s = tuple(mod_s[:, i] for i in range(N_MOD))
        w_in_b = w_in[l].astype(BF16)
        w_out_b = w_out[l].astype(BF16)
        lw = dict(
            alpha=alpha,
            w_q=w_in_b[:, :aw], w_kv=w_in_b[:, aw:aw + 2 * KV_WIDTH], w_u=w_in_b[:, aw + 2 * KV_WIDTH:],
            sinks=sinks[l], pool_w=pool_w[l].astype(BF16), pool_scale=pool_scale[l][None, :],
            w_out_a=w_out_b[:aw], w_out_p=w_out_b[aw:],
            ln1_g=ln1_g[l][None, :], ln1_b=ln1_b[l][None, :], ln2_g=ln2_g[l][None, :], ln2_b=ln2_b[l][None, :],
            peer_wq=peer_wq[l].astype(BF16), subkeys=peer_subkeys[l].astype(BF16),
            table=pack_expert_tables(peer_u[l], peer_v[l]),
        )
        xp, (k1, v1, p1) = _layer(xp, mods_p, lw, 0, 1, 512)
        xs, (k2, v2, p2) = _layer(xs, mods_s, lw, past_len, 64, t_new, prev=(cache_k[l], cache_v[l], state_pool[l]))
        for lst, val in zip(outs, (k1, v1, p1, k2, v2, p2)):
            lst.append(val)
    return (xp, xs) + tuple(jnp.stack(o) for o in outs)
```

```python
import functools

import jax
import jax.numpy as jnp
from jax import lax
from jax.experimental import pallas as pl
from jax.experimental.pallas import tpu as pltpu

F32 = jnp.float32
BF16 = jnp.bfloat16

LANES = 128
HEAD_DIM = 64
N_HEADS = 32
N_KV_HEADS = 4
GROUP = N_HEADS // N_KV_HEADS
KV_WIDTH = N_KV_HEADS * HEAD_DIM
WINDOW = 128
ROT_DIM = HEAD_DIM // 4
ROT_HALF = ROT_DIM // 2
ROPE_THETA = 500000.0
POOL_WINDOWS = (2, 4, 8, 16)
POOL_GROUP_WIDTH = 512
POOL_HALO = 16
POOL_STATE = 15
PEER_HEADS = 8
PEER_HALF = 128
N_KEYS = 128
PEER_TOPK = 16
PEER_K = PEER_HEADS * PEER_TOPK
N_MOD = 6
PAST_LEN = 8192
LN_ROWS = 128
POOL_TG = 16
LN_EPS = 1e-5
NEG_INF = -1e30


def _params(vmem_mb, sem=None):
    return pltpu.CompilerParams(dimension_semantics=sem, vmem_limit_bytes=vmem_mb << 20)


def _ada_kernel(c_ref, w_ref, b_ref, o_ref):
    c = c_ref[...]
    s = (c * jax.nn.sigmoid(c)).astype(BF16)
    o_ref[...] = jnp.dot(s, w_ref[...].astype(BF16), preferred_element_type=F32) + b_ref[...]


def adaln(c_all, w_ada, b_ada, tn=512):
    r, d = c_all.shape
    n = w_ada.shape[1]
    return pl.pallas_call(
        _ada_kernel,
        grid=(n // tn,),
        in_specs=[pl.BlockSpec((r, d), lambda j: (0, 0)),
                  pl.BlockSpec((d, tn), lambda j: (0, j)),
                  pl.BlockSpec((1, tn), lambda j: (0, j))],
        out_specs=pl.BlockSpec((r, tn), lambda j: (0, j)),
        out_shape=jax.ShapeDtypeStruct((r, n), F32),
        compiler_params=_params(48, ("arbitrary",)),
        name="adaln",
    )(c_all, w_ada, b_ada)


def _mod_kernel(x_ref, sc_ref, sh_ref, o_ref):
    h = x_ref[...] * (1.0 + sc_ref[...]) + sh_ref[...]
    o_ref[...] = h.reshape(o_ref.shape).astype(o_ref.dtype)


def modulate(x3, sc, sh, tg, tt):
    g, t, d = x3.shape
    nt = t // tt
    return pl.pallas_call(
        _mod_kernel,
        grid=(g // tg, nt),
        in_specs=[pl.BlockSpec((tg, tt, d), lambda a, b: (a, b, 0)),
                  pl.BlockSpec((tg, 1, d), lambda a, b: (a, 0, 0)),
                  pl.BlockSpec((tg, 1, d), lambda a, b: (a, 0, 0))],
        out_specs=pl.BlockSpec((tg * tt, d), lambda a, b: (a * nt + b, 0)),
        out_shape=jax.ShapeDtypeStruct((g * t, d), BF16),
        compiler_params=_params(48, ("arbitrary", "arbitrary")),
        name="modulate",
    )(x3, sc, sh)


def _mm_kernel(*refs, n_pairs):
    o_ref = refs[-1]
    acc = None
    for p in range(n_pairs):
        part = jnp.dot(refs[2 * p][...], refs[2 * p + 1][...], preferred_element_type=F32)
        acc = part if acc is None else acc + part
    o_ref[...] = acc.astype(o_ref.dtype)


def matmul(pairs, tm=1024, tn=512, out_dtype=F32, name="matmul"):
    m = pairs[0][0].shape[0]
    n = pairs[0][1].shape[1]
    tm, tn = min(tm, m), min(tn, n)
    in_specs, args = [], []
    for x, w in pairs:
        k = x.shape[1]
        in_specs += [pl.BlockSpec((tm, k), lambda i, j: (i, 0)), pl.BlockSpec((k, tn), lambda i, j: (0, j))]
        args += [x, w]
    return pl.pallas_call(
        functools.partial(_mm_kernel, n_pairs=len(pairs)),
        grid=(m // tm, n // tn),
        in_specs=in_specs,
        out_specs=pl.BlockSpec((tm, tn), lambda i, j: (i, j)),
        out_shape=jax.ShapeDtypeStruct((m, n), out_dtype),
        compiler_params=_params(48, ("arbitrary", "arbitrary")),
        name=name,
    )(*args)


def rope_tables(pos):
    inv_freq = ROPE_THETA ** (-jnp.arange(ROT_HALF, dtype=F32) / ROT_HALF)
    ang = pos.astype(F32)[:, None] * inv_freq[None, :]
    cos, sin = jnp.cos(ang), jnp.sin(ang)
    t = pos.shape[0]
    ones = jnp.ones((t, HEAD_DIM - ROT_DIM), F32)
    zeros_h = jnp.zeros((t, ROT_HALF), F32)
    zeros_r = jnp.zeros((t, HEAD_DIM - ROT_DIM), F32)
    c = jnp.concatenate([cos, cos, ones], axis=1)
    s1 = jnp.concatenate([zeros_h, sin, zeros_r], axis=1)
    s2 = jnp.concatenate([-sin, zeros_h, zeros_r], axis=1)
    rep = LANES // HEAD_DIM
    return tuple(jnp.tile(a, (1, rep)) for a in (c, s1, s2))


def _rope(x, c, s1, s2):
    rows, w = x.shape
    reps = (rows // c.shape[0], w // LANES)
    ct, s1t, s2t = (jnp.tile(a, reps) for a in (c, s1, s2))
    return x * ct + pltpu.roll(x, ROT_HALF, 1) * s1t + pltpu.roll(x, w - ROT_HALF, 1) * s2t


def _sink_softmax_pv(pieces, sink):
    m = sink
    for s, _ in pieces:
        m = jnp.maximum(m, s.max(-1, keepdims=True))
    denom = jnp.exp(sink - m)
    o = None
    for s, v in pieces:
        p = jnp.exp(s - m)
        denom = denom + p.sum(-1, keepdims=True)
        pv = jnp.dot(p.astype(BF16), v, preferred_element_type=F32)
        o = pv if o is None else o + pv
    return o / denom


_NT = (((1,), (1,)), ((), ()))


def _attn_prompt_kernel(q_ref, kp_ref, kc_ref, vp_ref, vc_ref, cc_ref, s1c_ref, s2c_ref,
                        cp_ref, s1p_ref, s2p_ref, sink_ref, o_ref, krot_ref):
    i = pl.program_id(1)
    q = (_rope(q_ref[...], cc_ref[...], s1c_ref[...], s2c_ref[...]) * (HEAD_DIM ** -0.5)).astype(BF16)
    kc = _rope(kc_ref[...], cc_ref[...], s1c_ref[...], s2c_ref[...])
    krot_ref[...] = kc
    kp = _rope(kp_ref[...], cp_ref[...], s1p_ref[...], s2p_ref[...])
    kc, kp = kc.astype(BF16), kp.astype(BF16)
    vc, vp = vc_ref[...].astype(BF16), vp_ref[...].astype(BF16)
    gr = GROUP * WINDOW
    r = lax.broadcasted_iota(jnp.int32, (gr, WINDOW), 0) % WINDOW
    j = lax.broadcasted_iota(jnp.int32, (gr, WINDOW), 1)
    mask_c = j <= r
    mask_p = (j >= r) & (i > 0)
    outs = []
    for h in range(N_KV_HEADS):
        hs = slice(h * HEAD_DIM, (h + 1) * HEAD_DIM)
        qg = jnp.concatenate([q[:, (h * GROUP + g) * HEAD_DIM:(h * GROUP + g + 1) * HEAD_DIM]
                              for g in range(GROUP)], axis=0)
        sink = jnp.concatenate([jnp.full((WINDOW, 1), sink_ref[h * GROUP + g], F32) for g in range(GROUP)], axis=0)
        sp = jnp.where(mask_p, lax.dot_general(qg, kp[:, hs], _NT, preferred_element_type=F32), NEG_INF)
        sc = jnp.where(mask_c, lax.dot_general(qg, kc[:, hs], _NT, preferred_element_type=F32), NEG_INF)
        o = _sink_softmax_pv([(sp, vp[:, hs]), (sc, vc[:, hs])], sink)
        outs += [o[g * WINDOW:(g + 1) * WINDOW, :] for g in range(GROUP)]
    o_ref[...] = jnp.concatenate(outs, axis=1).astype(o_ref.dtype)


def attn_prompt(q, kv, tabs, sinks, batch, seq):
    nb = seq // WINDOW
    aw = N_HEADS * HEAD_DIM
    cur = lambda b, i: (b * nb + i, 0)
    prev = lambda b, i: (b * nb + jnp.maximum(i - 1, 0), 0)
    cur_v = lambda b, i: (b * nb + i, 1)
    prev_v = lambda b, i: (b * nb + jnp.maximum(i - 1, 0), 1)
    tcur = lambda b, i: (i, 0)
    tprev = lambda b, i: (jnp.maximum(i - 1, 0), 0)
    tspec = lambda f: pl.BlockSpec((WINDOW, LANES), f)
    return pl.pallas_call(
        _attn_prompt_kernel,
        grid=(batch, nb),
        in_specs=[pl.BlockSpec((WINDOW, aw), cur),
                  pl.BlockSpec((WINDOW, KV_WIDTH), prev), pl.BlockSpec((WINDOW, KV_WIDTH), cur),
                  pl.BlockSpec((WINDOW, KV_WIDTH), prev_v), pl.BlockSpec((WINDOW, KV_WIDTH), cur_v),
                  tspec(tcur), tspec(tcur), tspec(tcur), tspec(tprev), tspec(tprev), tspec(tprev),
                  pl.BlockSpec(memory_space=pltpu.SMEM)],
        out_specs=[pl.BlockSpec((WINDOW, aw), cur), pl.BlockSpec((WINDOW, KV_WIDTH), cur)],
        out_shape=[jax.ShapeDtypeStruct((batch * seq, aw), BF16),
                   jax.ShapeDtypeStruct((batch * seq, KV_WIDTH), F32)],
        compiler_params=_params(32, ("arbitrary", "arbitrary")),
        name="attn_prompt",
    )(q, kv, kv, kv, kv, *tabs, *tabs, sinks)


def _attn_sample_kernel(q_ref, kv_ref, ck_ref, cv_ref, c_ref, s1_ref, s2_ref, sink_ref, o_ref, krot_ref,
                        qs_ref, os_ref, *, nb, t_new):
    tabs = (c_ref[...], s1_ref[...], s2_ref[...])
    qs_ref[...] = (_rope(q_ref[...], *tabs) * (HEAD_DIM ** -0.5)).astype(BF16).astype(F32)
    krot_ref[...] = _rope(kv_ref[:, :KV_WIDTH], *tabs)
    wc = ck_ref.shape[1]
    gr = GROUP * t_new
    tok = lax.broadcasted_iota(jnp.int32, (gr, wc), 0) % t_new
    diff_c = tok + wc - lax.broadcasted_iota(jnp.int32, (gr, wc), 1)
    mask_c = (diff_c >= 0) & (diff_c <= WINDOW)
    mask_n = (lax.broadcasted_iota(jnp.int32, (gr, t_new), 1)
              <= lax.broadcasted_iota(jnp.int32, (gr, t_new), 0) % t_new)
    sink_cols = [jnp.concatenate([jnp.full((t_new, 1), sink_ref[h * GROUP + g], F32) for g in range(GROUP)], axis=0)
                 for h in range(N_KV_HEADS)]

    def body(n, carry):
        rows = pl.ds(pl.multiple_of(n * t_new, t_new), t_new)
        q = qs_ref[rows, :]
        kn = krot_ref[rows, :].astype(BF16)
        vn = kv_ref[rows, KV_WIDTH:].astype(BF16)
        kc = ck_ref[n].astype(BF16)
        vc = cv_ref[n].astype(BF16)
        outs = []
        for h in range(N_KV_HEADS):
            hs = slice(h * HEAD_DIM, (h + 1) * HEAD_DIM)
            qg = jnp.concatenate([q[:, (h * GROUP + g) * HEAD_DIM:(h * GROUP + g + 1) * HEAD_DIM]
                                  for g in range(GROUP)], axis=0).astype(BF16)
            sc = jnp.where(mask_c, lax.dot_general(qg, kc[:, hs], _NT, preferred_element_type=F32), NEG_INF)
            sn = jnp.where(mask_n, lax.dot_general(qg, kn[:, hs], _NT, preferred_element_type=F32), NEG_INF)
            o = _sink_softmax_pv([(sc, vc[:, hs]), (sn, vn[:, hs])], sink_cols[h])
            outs += [o[g * t_new:(g + 1) * t_new, :] for g in range(GROUP)]
        os_ref[rows, :] = jnp.concatenate(outs, axis=1)
        return carry

    lax.fori_loop(0, nb, body, 0)
    o_ref[...] = os_ref[...].astype(o_ref.dtype)


def attn_sample(q, kv, cache_k, cache_v, tabs, sinks, t_new, nb=8):
    rows = q.shape[0]
    n = rows // t_new
    wc = cache_k.shape[1]
    aw = N_HEADS * HEAD_DIM
    tm = nb * t_new
    blk = lambda i: (i, 0)
    full = lambda i: (0, 0)
    return pl.pallas_call(
        functools.partial(_attn_sample_kernel, nb=nb, t_new=t_new),
        grid=(n // nb,),
        in_specs=[pl.BlockSpec((tm, aw), blk), pl.BlockSpec((tm, 2 * KV_WIDTH), blk),
                  pl.BlockSpec((nb, wc, KV_WIDTH), lambda i: (i, 0, 0)),
                  pl.BlockSpec((nb, wc, KV_WIDTH), lambda i: (i, 0, 0)),
                  pl.BlockSpec((t_new, LANES), full), pl.BlockSpec((t_new, LANES), full),
                  pl.BlockSpec((t_new, LANES), full),
                  pl.BlockSpec(memory_space=pltpu.SMEM)],
        out_specs=[pl.BlockSpec((tm, aw), blk), pl.BlockSpec((tm, KV_WIDTH), blk)],
        out_shape=[jax.ShapeDtypeStruct((rows, aw), BF16), jax.ShapeDtypeStruct((rows, KV_WIDTH), F32)],
        scratch_shapes=[pltpu.VMEM((tm, aw), F32), pltpu.VMEM((tm, aw), F32)],
        compiler_params=_params(32, ("arbitrary",)),
        name="attn_sample",
    )(q, kv, cache_k, cache_v, *tabs, sinks)


def _pool_kernel(u_ref, halo_ref, st_ref, pw_ref, scale_ref, o_ref, ext_ref, *, pos0):
    ti = pl.program_id(1)
    tg, tt, _ = u_ref.shape
    ext_ref[:, 0:POOL_HALO, :] = jnp.where(ti == 0, st_ref[...], halo_ref[...])
    ext_ref[:, POOL_HALO:, :] = u_ref[...]
    pos = pos0 + ti * tt + lax.broadcasted_iota(jnp.int32, (1, tt, 1), 1)
    for g, w in enumerate(POOL_WINDOWS):
        cols = slice(g * POOL_GROUP_WIDTH, (g + 1) * POOL_GROUP_WIDTH)
        acc = ext_ref[:, POOL_HALO:POOL_HALO + tt, cols]
        for k in range(1, w):
            acc = acc + ext_ref[:, POOL_HALO - k:POOL_HALO - k + tt, cols]
        inv_cnt = 1.0 / jnp.minimum(w, pos + 1).astype(F32)
        d = acc * inv_cnt - u_ref[:, :, cols]
        d = d.reshape(tg * tt, POOL_GROUP_WIDTH).astype(BF16)
        y = jnp.dot(d, pw_ref[g], preferred_element_type=F32) * scale_ref[:, cols]
        o_ref[:, cols] = y.astype(o_ref.dtype)


def pool_mix(u3, state16, pool_w, pool_scale, tg, tt, pos0):
    g, t, pw = u3.shape
    nt = t // tt
    hb = tt // POOL_HALO
    halo_src = u3 if nt > 1 else state16
    return pl.pallas_call(
        functools.partial(_pool_kernel, pos0=pos0),
        grid=(g // tg, nt),
        in_specs=[pl.BlockSpec((tg, tt, pw), lambda a, b: (a, b, 0)),
                  pl.BlockSpec((tg, POOL_HALO, pw), lambda a, b: (a, jnp.maximum(b * hb - 1, 0), 0)),
                  pl.BlockSpec((tg, POOL_HALO, pw), lambda a, b: (a, 0, 0)),
                  pl.BlockSpec(pool_w.shape, lambda a, b: (0, 0, 0)),
                  pl.BlockSpec((1, pw), lambda a, b: (0, 0))],
        out_specs=pl.BlockSpec((tg * tt, pw), lambda a, b: (a * nt + b, 0)),
        out_shape=jax.ShapeDtypeStruct((g * t, pw), BF16),
        scratch_shapes=[pltpu.VMEM((tg, POOL_HALO + tt, pw), F32)],
        compiler_params=_params(48, ("arbitrary", "arbitrary")),
        name="pool_mix",
    )(u3, halo_src, state16, pool_w, pool_scale)


def _layer_norm(z, g, b):
    mu = z.mean(-1, keepdims=True)
    zc = z - mu
    var = (zc * zc).mean(-1, keepdims=True)
    return zc * lax.rsqrt(var + LN_EPS) * g + b


def _ln1_kernel(x_ref, y_ref, g1_ref, sc_ref, sh_ref, lg_ref, lb_ref, x1_ref, h_ref, hb_ref, *, alpha):
    x = x_ref[...]
    z = alpha * x + g1_ref[...] * y_ref[...].reshape(x.shape)
    x1 = _layer_norm(z, lg_ref[...], lb_ref[...])
    h = x1 * (1.0 + sc_ref[...]) + sh_ref[...]
    x1_ref[...] = x1.reshape(x1_ref.shape)
    h2 = h.reshape(h_ref.shape)
    h_ref[...] = h2
    hb_ref[...] = h2.astype(BF16)


def ln1_modulate(x3, y, g1, sc, sh, ln_g, ln_b, tg, tt, alpha):
    g, t, d = x3.shape
    nt = t // tt
    rows = lambda a, b: (a * nt + b, 0)
    grp = lambda a, b: (a, 0, 0)
    vec = lambda a, b: (0, 0)
    return pl.pallas_call(
        functools.partial(_ln1_kernel, alpha=alpha),
        grid=(g // tg, nt),
        in_specs=[pl.BlockSpec((tg, tt, d), lambda a, b: (a, b, 0)), pl.BlockSpec((tg * tt, d), rows),
                  pl.BlockSpec((tg, 1, d), grp), pl.BlockSpec((tg, 1, d), grp), pl.BlockSpec((tg, 1, d), grp),
                  pl.BlockSpec((1, d), vec), pl.BlockSpec((1, d), vec)],
        out_specs=[pl.BlockSpec((tg * tt, d), rows)] * 3,
        out_shape=[jax.ShapeDtypeStruct((g * t, d), F32), jax.ShapeDtypeStruct((g * t, d), F32),
                   jax.ShapeDtypeStruct((g * t, d), BF16)],
        compiler_params=_params(48, ("arbitrary", "arbitrary")),
        name="ln1_modulate",
    )(x3, y, g1, sc, sh, ln_g, ln_b)


def _topk_rows(vals, k, payload=None):
    n = vals.shape[0]
    iota = lax.broadcasted_iota(jnp.int32, vals.shape, 0).astype(F32)
    out_v, out_i = [], []
    for _ in range(k):
        m = jnp.max(vals, axis=0, keepdims=True)
        am = jnp.min(jnp.where(vals == m, iota, float(n)), axis=0, keepdims=True)
        hit = iota == am
        out_v.append(m)
        out_i.append(am if payload is None else jnp.sum(jnp.where(hit, payload, 0.0), axis=0, keepdims=True))
        vals = jnp.where(hit, -jnp.inf, vals)
    return jnp.concatenate(out_v, axis=0), jnp.concatenate(out_i, axis=0)


def _peer_select_kernel(q_ref, sk_ref, idx_ref, gate_ref):
    tm = q_ref.shape[0]
    idx_rows, gate_rows = [], []
    for h in range(PEER_HEADS):
        tops = []
        for p in range(2):
            c0 = (h * 2 + p) * PEER_HALF
            qhp = q_ref[:, c0:c0 + PEER_HALF].astype(BF16)
            s_t = lax.dot_general(sk_ref[h, p], qhp, _NT, preferred_element_type=F32)
            tops.append(_topk_rows(s_t, PEER_TOPK))
        (v1, i1), (v2, i2) = tops
        nb = [PEER_TOPK // (a + 1) for a in range(PEER_TOPK)]
        pad = (-sum(nb)) % 8
        cand = jnp.concatenate([v1[a:a + 1, :] + v2[:nb[a], :] for a in range(PEER_TOPK)]
                               + [jnp.full((pad, tm), -jnp.inf, F32)], axis=0)
        cidx = jnp.concatenate([i1[a:a + 1, :] * float(N_KEYS) + i2[:nb[a], :] for a in range(PEER_TOPK)]
                               + [jnp.zeros((pad, tm), F32)], axis=0)
        best, eidx = _topk_rows(cand, PEER_TOPK, payload=cidx)
        e = jnp.exp(best - best[0:1, :])
        gate_rows.append(e / e.sum(axis=0, keepdims=True))
        idx_rows.append(eidx)
    idx_ref[...] = jnp.concatenate(idx_rows, axis=0).T.astype(jnp.int32)
    gate_ref[...] = jnp.concatenate(gate_rows, axis=0).T


def peer_select(qp, subkeys_bf16, tm=128):
    rows, w = qp.shape
    tm = min(tm, rows)
    blk = lambda i: (i, 0)
    return pl.pallas_call(
        _peer_select_kernel,
        grid=(rows // tm,),
        in_specs=[pl.BlockSpec((tm, w), blk), pl.BlockSpec(subkeys_bf16.shape, lambda i: (0, 0, 0, 0))],
        out_specs=[pl.BlockSpec((tm, PEER_K), blk), pl.BlockSpec((tm, PEER_K), blk)],
        out_shape=[jax.ShapeDtypeStruct((rows, PEER_K), jnp.int32), jax.ShapeDtypeStruct((rows, PEER_K), F32)],
        compiler_params=_params(32, ("arbitrary",)),
        name="peer_select",
    )(qp, subkeys_bf16)


def _pack_kernel(u_ref, v_ref, o_ref):
    half = u_ref.shape[1] // 2

    def pack(ref):
        bits = lax.bitcast_convert_type(ref[...].astype(BF16).astype(F32), jnp.uint32)
        return (bits[:, :half] >> 16) | (bits[:, half:] & jnp.uint32(0xFFFF0000))

    o_ref[...] = jnp.concatenate([pack(u_ref), pack(v_ref)], axis=1).reshape(o_ref.shape)


def pack_expert_tables(peer_u, peer_v, tr=256):
    e, d = peer_u.shape
    blk = pl.BlockSpec((tr, d), lambda i: (i, 0))
    return pl.pallas_call(
        _pack_kernel,
        grid=(e // tr,),
        in_specs=[blk, blk],
        out_specs=pl.BlockSpec((tr, 1, d), lambda i: (i, 0, 0)),
        out_shape=jax.ShapeDtypeStruct((e, 1, d), jnp.uint32),
        compiler_params=_params(48, ("arbitrary",)),
        name="pack_tables",
    )(peer_u, peer_v)


def _unpack(w):
    lo = lax.bitcast_convert_type(w << 16, F32)
    hi = lax.bitcast_convert_type(w & jnp.uint32(0xFFFF0000), F32)
    return lo, hi


def _peer_mix_kernel(idx_ref, nidx_ref, x_ref, gate_ref, tab_ref, x1_ref, g2_ref, lg_ref, lb_ref, o_ref,
                     buf_a, buf_b, sem_a, sem_b, y_ref, *, tt, nsteps, alpha):
    i = pl.program_id(0)
    half = x_ref.shape[1] // 2
    nchunk = half // LANES

    def issue(ids, row, buf, sem, t):
        for k in range(PEER_K):
            pltpu.make_async_copy(tab_ref.at[ids[row, k]], buf.at[t, pl.ds(k, 1), :], sem.at[t]).start()

    def wait(buf, sem, t):
        pltpu.make_async_copy(buf.at[t], buf.at[t], sem.at[t]).wait()

    @pl.when(i == 0)
    def _():
        def first(t, c):
            issue(idx_ref, t, buf_a, sem_a, t)
            return c
        lax.fori_loop(0, tt, first, 0)

    eye = (lax.broadcasted_iota(jnp.int32, (PEER_K, PEER_K), 0)
           == lax.broadcasted_iota(jnp.int32, (PEER_K, PEER_K), 1))
    row_id = lax.broadcasted_iota(jnp.int32, (tt, LANES), 0)
    y_ref[...] = jnp.zeros(y_ref.shape, F32)

    def mix(buf, t, row0):
        x = x_ref[pl.ds(row0 + t, 1), :]
        acc = jnp.zeros((PEER_K, LANES), F32)
        for ch in range(nchunk):
            cs = slice(ch * LANES, (ch + 1) * LANES)
            lo, hi = _unpack(buf[t, :, cs])
            acc = acc + lo * x[:, cs] + hi * x[:, half + ch * LANES:half + (ch + 1) * LANES]
        a = acc.sum(axis=1, keepdims=True)
        gate = jnp.sum(jnp.where(eye, gate_ref[pl.ds(row0 + t, 1), :], 0.0), axis=1, keepdims=True)
        act = 0.5 * a * (1.0 + lax.erf(a * (2.0 ** -0.5))) * gate
        act_b = jnp.broadcast_to(act, (PEER_K, LANES))
        mine = row_id == t
        rows = slice(row0, row0 + tt)
        for ch in range(nchunk):
            lo, hi = _unpack(buf[t, :, half + ch * LANES:half + (ch + 1) * LANES])
            for part, c0 in ((lo, ch * LANES), (hi, half + ch * LANES)):
                cs = slice(c0, c0 + LANES)
                y_ref[rows, cs] = jnp.where(mine, (act_b * part).sum(axis=0, keepdims=True), y_ref[rows, cs])

    def token_a(t, c):
        wait(buf_a, sem_a, t)
        issue(idx_ref, tt + t, buf_b, sem_b, t)
        mix(buf_a, t, 0)
        return c

    def token_b(t, c):
        wait(buf_b, sem_b, t)
        issue(nidx_ref, t, buf_a, sem_a, t)
        mix(buf_b, t, tt)
        return c

    lax.fori_loop(0, tt, token_a, 0)
    lax.fori_loop(0, tt, token_b, 0)

    @pl.when(i == nsteps - 1)
    def _():
        for t in range(tt):
            wait(buf_a, sem_a, t)

    z = alpha * x1_ref[...] + g2_ref[...] * y_ref[...].reshape(x1_ref.shape)
    o_ref[...] = _layer_norm(z, lg_ref[...], lb_ref[...])


def peer_mix_ln2(h2, eidx, gate, table, x1_3, g2, ln_g, ln_b, alpha, tt=8):
    g, t, d = x1_3.shape
    rows = g * t
    tb = 2 * tt
    nsteps = rows // tb
    bt = min(tb, t)
    bg = tb // bt
    nt = t // bt
    blk = lambda i: (i, 0)
    nxt = lambda i: ((i + 1) % nsteps, 0)
    tok3 = pl.BlockSpec((bg, bt, d), lambda i: (i // nt, i % nt, 0))
    vec = pl.BlockSpec((1, d), lambda i: (0, 0))
    return pl.pallas_call(
        functools.partial(_peer_mix_kernel, tt=tt, nsteps=nsteps, alpha=alpha),
        grid=(nsteps,),
        in_specs=[pl.BlockSpec((tb, PEER_K), blk, memory_space=pltpu.SMEM),
                  pl.BlockSpec((tb, PEER_K), nxt, memory_space=pltpu.SMEM),
                  pl.BlockSpec((tb, d), blk), pl.BlockSpec((tb, PEER_K), blk),
                  pl.BlockSpec(memory_space=pl.ANY),
                  tok3, pl.BlockSpec((bg, 1, d), lambda i: (i // nt, 0, 0)), vec, vec],
        out_specs=tok3,
        out_shape=jax.ShapeDtypeStruct((g, t, d), F32),
        scratch_shapes=[pltpu.VMEM((tt, PEER_K, d), jnp.uint32), pltpu.VMEM((tt, PEER_K, d), jnp.uint32),
                        pltpu.SemaphoreType.DMA((tt,)), pltpu.SemaphoreType.DMA((tt,)),
                        pltpu.VMEM((tb, d), F32)],
        compiler_params=_params(56, ("arbitrary",)),
        name="peer_mix",
    )(eidx, eidx, h2, gate, table, x1_3, g2, ln_g, ln_b)


def _layer(x3, mods, lw, pos0, tg, tt, prev=None):
    g, t, d = x3.shape
    sh1, sc1, g1, sh2, sc2, g2 = mods
    aw = N_HEADS * HEAD_DIM
    alpha = lw["alpha"]

    h = modulate(x3, sc1, sh1, tg, tt)
    q = matmul([(h, lw["w_q"])], name="in_q")
    kv = matmul([(h, lw["w_kv"])], name="in_kv")
    u = matmul([(h, lw["w_u"])], name="in_u")

    tabs = rope_tables(pos0 + jnp.arange(t, dtype=jnp.int32))
    if prev is None:
        attn, krot = attn_prompt(q, kv, tabs, lw["sinks"], g, t)
        state16 = jnp.zeros((g, POOL_HALO, u.shape[1]), F32)
        new_k = krot.reshape(g, t, KV_WIDTH)[:, -WINDOW:].reshape(g, WINDOW, N_KV_HEADS, HEAD_DIM)
        new_v = kv.reshape(g, t, 2 * KV_WIDTH)[:, -WINDOW:, KV_WIDTH:].reshape(g, WINDOW, N_KV_HEADS, HEAD_DIM)
    else:
        cache_k, cache_v, state = prev
        wc = cache_k.shape[1]
        attn, krot = attn_sample(q, kv, cache_k.reshape(g, wc, KV_WIDTH), cache_v.reshape(g, wc, KV_WIDTH),
                                 tabs, lw["sinks"], t)
        state16 = jnp.pad(state, ((0, 0), (POOL_HALO - POOL_STATE, 0), (0, 0)))
        new_k = jnp.concatenate([cache_k, krot.reshape(g, t, N_KV_HEADS, HEAD_DIM)], axis=1)[:, -wc:]
        new_v = jnp.concatenate([cache_v, kv[:, KV_WIDTH:].reshape(g, t, N_KV_HEADS, HEAD_DIM)], axis=1)[:, -wc:]
    u3 = u.reshape(g, t, u.shape[1])
    new_pool = jnp.concatenate([state16, u3], axis=1)[:, -POOL_STATE:]
    pooled = pool_mix(u3, state16, lw["pool_w"], lw["pool_scale"], min(tg, POOL_TG), tt, pos0)

    y1 = matmul([(attn, lw["w_out_a"]), (pooled, lw["w_out_p"])], name="out_proj")
    ln_tt = min(t, LN_ROWS)
    ln_tg = LN_ROWS // ln_tt
    x1, h2, h2b = ln1_modulate(x3, y1, g1, sc2, sh2, lw["ln1_g"], lw["ln1_b"], ln_tg, ln_tt, alpha)

    qp = matmul([(h2b, lw["peer_wq"])], name="peer_query")
    eidx, gate = peer_select(qp, lw["subkeys"])
    out = peer_mix_ln2(h2, eidx, gate, lw["table"], x1.reshape(g, t, d), g2, lw["ln2_g"], lw["ln2_b"], alpha)
    return out, (new_k, new_v, new_pool)


def kernel(x_prompt, x_sample, cache_k, cache_v, state_pool, c_prompt, c_sample, w_ada, b_ada, w_in, sinks,
           pool_w, pool_scale, w_out, ln1_g, ln1_b, peer_wq, peer_subkeys, peer_u, peer_v, ln2_g, ln2_b):
    depth = w_ada.shape[0]
    bp, seq, d = x_prompt.shape
    bs, t_new, _ = x_sample.shape
    past_len = PAST_LEN
    alpha = (2 * depth) ** 0.25
    aw = N_HEADS * HEAD_DIM
    n_c = bp + bs
    pad = (-n_c) % 8

    xp, xs = x_prompt, x_sample
    outs = [[] for _ in range(6)]
    for l in range(depth):
        c_all = jnp.concatenate([c_prompt, c_sample, jnp.zeros((pad, d), F32)], axis=0)
        mod = adaln(c_all, w_ada[l], b_ada[l][None, :])
        mod_p = mod[:bp].reshape(bp, N_MOD, 1, d)
        mod_s = mod[bp:n_c].reshape(bs, N_MOD, 1, d)
        mods_p = tuple(mod_p[:, i] for i in range(N_MOD))
        mods_s = tuple(mod_s[:, i] for i in range(N_MOD))
        w_in_b = w_in[l].astype(BF16)
        w_out_b = w_out[l].astype(BF16)
        lw = dict(
            alpha=alpha,
            w_q=w_in_b[:, :aw], w_kv=w_in_b[:, aw:aw + 2 * KV_WIDTH], w_u=w_in_b[:, aw + 2 * KV_WIDTH:],
            sinks=sinks[l], pool_w=pool_w[l].astype(BF16), pool_scale=pool_scale[l][None, :],
            w_out_a=w_out_b[:aw], w_out_p=w_out_b[aw:],
            ln1_g=ln1_g[l][None, :], ln1_b=ln1_b[l][None, :], ln2_g=ln2_g[l][None, :], ln2_b=ln2_b[l][None, :],
            peer_wq=peer_wq[l].astype(BF16), subkeys=peer_subkeys[l].astype(BF16),
            table=pack_expert_tables(peer_u[l], peer_v[l]),
        )
        xp, (k1, v1, p1) = _layer(xp, mods_p, lw, 0, 1, 512)
        xs, (k2, v2, p2) = _layer(xs, mods_s, lw, past_len, 64, t_new, prev=(cache_k[l], cache_v[l], state_pool[l]))
        for lst, val in zip(outs, (k1, v1, p1, k2, v2, p2)):
            lst.append(val)
    return (xp, xs) + tuple(jnp.stack(o) for o in outs)
```

```python
import functools

import jax
import jax.numpy as jnp
from jax import lax
from jax.experimental import pallas as pl
from jax.experimental.pallas import tpu as pltpu

F32 = jnp.float32
BF16 = jnp.bfloat16

LANES = 128
HEAD_DIM = 64
N_HEADS = 32
N_KV_HEADS = 4
GROUP = N_HEADS // N_KV_HEADS
KV_WIDTH = N_KV_HEADS * HEAD_DIM
WINDOW = 128
ROT_DIM = HEAD_DIM // 4
ROT_HALF = ROT_DIM // 2
ROPE_THETA = 500000.0
POOL_WINDOWS = (2, 4, 8, 16)
POOL_GROUP_WIDTH = 512
POOL_HALO = 16
POOL_STATE = 15
PEER_HEADS = 8
PEER_HALF = 128
N_KEYS = 128
PEER_TOPK = 16
PEER_K = PEER_HEADS * PEER_TOPK
N_MOD = 6
PAST_LEN = 8192
LN_ROWS = 128
POOL_TG = 16
LN_EPS = 1e-5
NEG_INF = -1e30


def _params(vmem_mb, sem=None):
    return pltpu.CompilerParams(dimension_semantics=sem, vmem_limit_bytes=vmem_mb << 20)


def _ada_kernel(c_ref, w_ref, b_ref, o_ref):
    c = c_ref[...]
    s = (c * jax.nn.sigmoid(c)).astype(BF16)
    o_ref[...] = jnp.dot(s, w_ref[...].astype(BF16), preferred_element_type=F32) + b_ref[...]


def adaln(c_all, w_ada, b_ada, tn=512):
    r, d = c_all.shape
    n = w_ada.shape[1]
    return pl.pallas_call(
        _ada_kernel,
        grid=(n // tn,),
        in_specs=[pl.BlockSpec((r, d), lambda j: (0, 0)),
                  pl.BlockSpec((d, tn), lambda j: (0, j)),
                  pl.BlockSpec((1, tn), lambda j: (0, j))],
        out_specs=pl.BlockSpec((r, tn), lambda j: (0, j)),
        out_shape=jax.ShapeDtypeStruct((r, n), F32),
        compiler_params=_params(48, ("arbitrary",)),
        name="adaln",
    )(c_all, w_ada, b_ada)


def _mod_kernel(x_ref, sc_ref, sh_ref, o_ref):
    h = x_ref[...] * (1.0 + sc_ref[...]) + sh_ref[...]
    o_ref[...] = h.reshape(o_ref.shape).astype(o_ref.dtype)


def modulate(x3, sc, sh, tg, tt):
    g, t, d = x3.shape
    nt = t // tt
    return pl.pallas_call(
        _mod_kernel,
        grid=(g // tg, nt),
        in_specs=[pl.BlockSpec((tg, tt, d), lambda a, b: (a, b, 0)),
                  pl.BlockSpec((tg, 1, d), lambda a, b: (a, 0, 0)),
                  pl.BlockSpec((tg, 1, d), lambda a, b: (a, 0, 0))],
        out_specs=pl.BlockSpec((tg * tt, d), lambda a, b: (a * nt + b, 0)),
        out_shape=jax.ShapeDtypeStruct((g * t, d), BF16),
        compiler_params=_params(48, ("arbitrary", "arbitrary")),
        name="modulate",
    )(x3, sc, sh)


def _mm_kernel(*refs, n_pairs):
    o_ref = refs[-1]
    acc = None
    for p in range(n_pairs):
        part = jnp.dot(refs[2 * p][...], refs[2 * p + 1][...].astype(BF16), preferred_element_type=F32)
        acc = part if acc is None else acc + part
    o_ref[...] = acc.astype(o_ref.dtype)


def matmul(pairs, n, tm=1024, tn=512, out_dtype=F32, name="matmul"):
    m = pairs[0][0].shape[0]
    tm, tn = min(tm, m), min(tn, n)
    in_specs, args = [], []
    for x, w, r, c in pairs:
        k = x.shape[1]
        assert c % tn == 0 and w.shape[0] % k == 0
        cb = c // tn
        in_specs += [pl.BlockSpec((tm, k), lambda i, j: (i, 0)),
                     pl.BlockSpec((k, tn), lambda i, j, r=r, cb=cb: (r, cb + j))]
        args += [x, w]
    return pl.pallas_call(
        functools.partial(_mm_kernel, n_pairs=len(pairs)),
        grid=(m // tm, n // tn),
        in_specs=in_specs,
        out_specs=pl.BlockSpec((tm, tn), lambda i, j: (i, j)),
        out_shape=jax.ShapeDtypeStruct((m, n), out_dtype),
        compiler_params=_params(48, ("arbitrary", "arbitrary")),
        name=name,
    )(*args)


def rope_tables(pos):
    inv_freq = ROPE_THETA ** (-jnp.arange(ROT_HALF, dtype=F32) / ROT_HALF)
    ang = pos.astype(F32)[:, None] * inv_freq[None, :]
    cos, sin = jnp.cos(ang), jnp.sin(ang)
    t = pos.shape[0]
    ones = jnp.ones((t, HEAD_DIM - ROT_DIM), F32)
    zeros_h = jnp.zeros((t, ROT_HALF), F32)
    zeros_r = jnp.zeros((t, HEAD_DIM - ROT_DIM), F32)
    c = jnp.concatenate([cos, cos, ones], axis=1)
    s1 = jnp.concatenate([zeros_h, sin, zeros_r], axis=1)
    s2 = jnp.concatenate([-sin, zeros_h, zeros_r], axis=1)
    rep = LANES // HEAD_DIM
    return tuple(jnp.tile(a, (1, rep)) for a in (c, s1, s2))


def _rope(x, c, s1, s2):
    rows, w = x.shape
    reps = (rows // c.shape[0], w // LANES)
    ct, s1t, s2t = (jnp.tile(a, reps) for a in (c, s1, s2))
    return x * ct + pltpu.roll(x, ROT_HALF, 1) * s1t + pltpu.roll(x, w - ROT_HALF, 1) * s2t


def _sink_softmax_pv(pieces, sink):
    m = sink
    for s, _ in pieces:
        m = jnp.maximum(m, s.max(-1, keepdims=True))
    denom = jnp.exp(sink - m)
    o = None
    for s, v in pieces:
        p = jnp.exp(s - m)
        denom = denom + p.sum(-1, keepdims=True)
        pv = jnp.dot(p.astype(BF16), v, preferred_element_type=F32)
        o = pv if o is None else o + pv
    return o / denom


_NT = (((1,), (1,)), ((), ()))


def _attn_prompt_kernel(q_ref, kp_ref, kc_ref, vp_ref, vc_ref, cc_ref, s1c_ref, s2c_ref,
                        cp_ref, s1p_ref, s2p_ref, sink_ref, o_ref, krot_ref):
    i = pl.program_id(1)
    q = (_rope(q_ref[...], cc_ref[...], s1c_ref[...], s2c_ref[...]) * (HEAD_DIM ** -0.5)).astype(BF16)
    kc = _rope(kc_ref[...], cc_ref[...], s1c_ref[...], s2c_ref[...])
    krot_ref[...] = kc
    kp = _rope(kp_ref[...], cp_ref[...], s1p_ref[...], s2p_ref[...])
    kc, kp = kc.astype(BF16), kp.astype(BF16)
    vc, vp = vc_ref[...].astype(BF16), vp_ref[...].astype(BF16)
    gr = GROUP * WINDOW
    r = lax.broadcasted_iota(jnp.int32, (gr, WINDOW), 0) % WINDOW
    j = lax.broadcasted_iota(jnp.int32, (gr, WINDOW), 1)
    mask_c = j <= r
    mask_p = (j >= r) & (i > 0)
    outs = []
    for h in range(N_KV_HEADS):
        hs = slice(h * HEAD_DIM, (h + 1) * HEAD_DIM)
        qg = jnp.concatenate([q[:, (h * GROUP + g) * HEAD_DIM:(h * GROUP + g + 1) * HEAD_DIM]
                              for g in range(GROUP)], axis=0)
        sink = jnp.concatenate([jnp.full((WINDOW, 1), sink_ref[h * GROUP + g], F32) for g in range(GROUP)], axis=0)
        sp = jnp.where(mask_p, lax.dot_general(qg, kp[:, hs], _NT, preferred_element_type=F32), NEG_INF)
        sc = jnp.where(mask_c, lax.dot_general(qg, kc[:, hs], _NT, preferred_element_type=F32), NEG_INF)
        o = _sink_softmax_pv([(sp, vp[:, hs]), (sc, vc[:, hs])], sink)
        outs += [o[g * WINDOW:(g + 1) * WINDOW, :] for g in range(GROUP)]
    o_ref[...] = jnp.concatenate(outs, axis=1).astype(o_ref.dtype)


def attn_prompt(q, kv, tabs, sinks, batch, seq):
    nb = seq // WINDOW
    aw = N_HEADS * HEAD_DIM
    cur = lambda b, i: (b * nb + i, 0)
    prev = lambda b, i: (b * nb + jnp.maximum(i - 1, 0), 0)
    cur_v = lambda b, i: (b * nb + i, 1)
    prev_v = lambda b, i: (b * nb + jnp.maximum(i - 1, 0), 1)
    tcur = lambda b, i: (i, 0)
    tprev = lambda b, i: (jnp.maximum(i - 1, 0), 0)
    tspec = lambda f: pl.BlockSpec((WINDOW, LANES), f)
    return pl.pallas_call(
        _attn_prompt_kernel,
        grid=(batch, nb),
        in_specs=[pl.BlockSpec((WINDOW, aw), cur),
                  pl.BlockSpec((WINDOW, KV_WIDTH), prev), pl.BlockSpec((WINDOW, KV_WIDTH), cur),
                  pl.BlockSpec((WINDOW, KV_WIDTH), prev_v), pl.BlockSpec((WINDOW, KV_WIDTH), cur_v),
                  tspec(tcur), tspec(tcur), tspec(tcur), tspec(tprev), tspec(tprev), tspec(tprev),
                  pl.BlockSpec(memory_space=pltpu.SMEM)],
        out_specs=[pl.BlockSpec((WINDOW, aw), cur), pl.BlockSpec((WINDOW, KV_WIDTH), cur)],
        out_shape=[jax.ShapeDtypeStruct((batch * seq, aw), BF16),
                   jax.ShapeDtypeStruct((batch * seq, KV_WIDTH), F32)],
        compiler_params=_params(32, ("arbitrary", "arbitrary")),
        name="attn_prompt",
    )(q, kv, kv, kv, kv, *tabs, *tabs, sinks)


def _attn_sample_kernel(q_ref, kv_ref, ck_ref, cv_ref, c_ref, s1_ref, s2_ref, sink_ref, o_ref, krot_ref,
                        qs_ref, os_ref, *, nb, t_new):
    tabs = (c_ref[...], s1_ref[...], s2_ref[...])
    qs_ref[...] = (_rope(q_ref[...], *tabs) * (HEAD_DIM ** -0.5)).astype(BF16).astype(F32)
    krot_ref[...] = _rope(kv_ref[:, :KV_WIDTH], *tabs)
    wc = ck_ref.shape[1]
    gr = GROUP * t_new
    tok = lax.broadcasted_iota(jnp.int32, (gr, wc), 0) % t_new
    diff_c = tok + wc - lax.broadcasted_iota(jnp.int32, (gr, wc), 1)
    mask_c = (diff_c >= 0) & (diff_c <= WINDOW)
    mask_n = (lax.broadcasted_iota(jnp.int32, (gr, t_new), 1)
              <= lax.broadcasted_iota(jnp.int32, (gr, t_new), 0) % t_new)
    sink_cols = [jnp.concatenate([jnp.full((t_new, 1), sink_ref[h * GROUP + g], F32) for g in range(GROUP)], axis=0)
                 for h in range(N_KV_HEADS)]

    def one(n):
        rows = pl.ds(pl.multiple_of(n * t_new, t_new), t_new)
        q = qs_ref[rows, :]
        kn = krot_ref[rows, :].astype(BF16)
        vn = kv_ref[rows, KV_WIDTH:].astype(BF16)
        kc = ck_ref[n].astype(BF16)
        vc = cv_ref[n].astype(BF16)
        outs = []
        for h in range(N_KV_HEADS):
            hs = slice(h * HEAD_DIM, (h + 1) * HEAD_DIM)
            qg = jnp.concatenate([q[:, (h * GROUP + g) * HEAD_DIM:(h * GROUP + g + 1) * HEAD_DIM]
                                  for g in range(GROUP)], axis=0).astype(BF16)
            sc = jnp.where(mask_c, lax.dot_general(qg, kc[:, hs], _NT, preferred_element_type=F32), NEG_INF)
            sn = jnp.where(mask_n, lax.dot_general(qg, kn[:, hs], _NT, preferred_element_type=F32), NEG_INF)
            o = _sink_softmax_pv([(sc, vc[:, hs]), (sn, vn[:, hs])], sink_cols[h])
            outs += [o[g * t_new:(g + 1) * t_new, :] for g in range(GROUP)]
        os_ref[rows, :] = jnp.concatenate(outs, axis=1)

    def body(n2, carry):
        one(2 * n2)
        one(2 * n2 + 1)
        return carry

    lax.fori_loop(0, nb // 2, body, 0)
    o_ref[...] = os_ref[...].astype(o_ref.dtype)


def attn_sample(q, kv, cache_k, cache_v, tabs, sinks, t_new, nb=8):
    rows = q.shape[0]
    n = rows // t_new
    wc = cache_k.shape[1]
    aw = N_HEADS * HEAD_DIM
    tm = nb * t_new
    blk = lambda i: (i, 0)
    full = lambda i: (0, 0)
    return pl.pallas_call(
        functools.partial(_attn_sample_kernel, nb=nb, t_new=t_new),
        grid=(n // nb,),
        in_specs=[pl.BlockSpec((tm, aw), blk), pl.BlockSpec((tm, 2 * KV_WIDTH), blk),
                  pl.BlockSpec((nb, wc, KV_WIDTH), lambda i: (i, 0, 0)),
                  pl.BlockSpec((nb, wc, KV_WIDTH), lambda i: (i, 0, 0)),
                  pl.BlockSpec((t_new, LANES), full), pl.BlockSpec((t_new, LANES), full),
                  pl.BlockSpec((t_new, LANES), full),
                  pl.BlockSpec(memory_space=pltpu.SMEM)],
        out_specs=[pl.BlockSpec((tm, aw), blk), pl.BlockSpec((tm, KV_WIDTH), blk)],
        out_shape=[jax.ShapeDtypeStruct((rows, aw), BF16), jax.ShapeDtypeStruct((rows, KV_WIDTH), F32)],
        scratch_shapes=[pltpu.VMEM((tm, aw), F32), pltpu.VMEM((tm, aw), F32)],
        compiler_params=_params(32, ("arbitrary",)),
        name="attn_sample",
    )(q, kv, cache_k, cache_v, *tabs, sinks)


def _pool_kernel(u_ref, halo_ref, st_ref, pw_ref, scale_ref, o_ref, ext_ref, *, pos0):
    ti = pl.program_id(1)
    tg, tt, _ = u_ref.shape
    ext_ref[:, 0:POOL_HALO, :] = jnp.where(ti == 0, st_ref[...], halo_ref[...])
    ext_ref[:, POOL_HALO:, :] = u_ref[...]
    pos = pos0 + ti * tt + lax.broadcasted_iota(jnp.int32, (1, tt, 1), 1)
    for g, w in enumerate(POOL_WINDOWS):
        cols = slice(g * POOL_GROUP_WIDTH, (g + 1) * POOL_GROUP_WIDTH)
        acc = ext_ref[:, POOL_HALO:POOL_HALO + tt, cols]
        for k in range(1, w):
            acc = acc + ext_ref[:, POOL_HALO - k:POOL_HALO - k + tt, cols]
        inv_cnt = 1.0 / jnp.minimum(w, pos + 1).astype(F32)
        d = acc * inv_cnt - u_ref[:, :, cols]
        d = d.reshape(tg * tt, POOL_GROUP_WIDTH).astype(BF16)
        y = jnp.dot(d, pw_ref[g], preferred_element_type=F32) * scale_ref[:, cols]
        o_ref[:, cols] = y.astype(o_ref.dtype)


def pool_mix(u3, state16, pool_w, pool_scale, tg, tt, pos0):
    g, t, pw = u3.shape
    nt = t // tt
    hb = tt // POOL_HALO
    halo_src = u3 if nt > 1 else state16
    return pl.pallas_call(
        functools.partial(_pool_kernel, pos0=pos0),
        grid=(g // tg, nt),
        in_specs=[pl.BlockSpec((tg, tt, pw), lambda a, b: (a, b, 0)),
                  pl.BlockSpec((tg, POOL_HALO, pw), lambda a, b: (a, jnp.maximum(b * hb - 1, 0), 0)),
                  pl.BlockSpec((tg, POOL_HALO, pw), lambda a, b: (a, 0, 0)),
                  pl.BlockSpec(pool_w.shape, lambda a, b: (0, 0, 0)),
                  pl.BlockSpec((1, pw), lambda a, b: (0, 0))],
        out_specs=pl.BlockSpec((tg * tt, pw), lambda a, b: (a * nt + b, 0)),
        out_shape=jax.ShapeDtypeStruct((g * t, pw), BF16),
        scratch_shapes=[pltpu.VMEM((tg, POOL_HALO + tt, pw), F32)],
        compiler_params=_params(48, ("arbitrary", "arbitrary")),
        name="pool_mix",
    )(u3, halo_src, state16, pool_w, pool_scale)


def _layer_norm(z, g, b):
    mu = z.mean(-1, keepdims=True)
    zc = z - mu
    var = (zc * zc).mean(-1, keepdims=True)
    return zc * lax.rsqrt(var + LN_EPS) * g + b


def _ln1_kernel(x_ref, y_ref, g1_ref, sc_ref, sh_ref, lg_ref, lb_ref, x1_ref, hb_ref, *, alpha):
    x = x_ref[...]
    z = alpha * x + g1_ref[...] * y_ref[...].reshape(x.shape)
    x1 = _layer_norm(z, lg_ref[...], lb_ref[...])
    h = x1 * (1.0 + sc_ref[...]) + sh_ref[...]
    x1_ref[...] = x1.reshape(x1_ref.shape)
    hb_ref[...] = h.reshape(hb_ref.shape).astype(BF16)


def ln1_modulate(x3, y, g1, sc, sh, ln_g, ln_b, tg, tt, alpha):
    g, t, d = x3.shape
    nt = t // tt
    rows = lambda a, b: (a * nt + b, 0)
    grp = lambda a, b: (a, 0, 0)
    vec = lambda a, b: (0, 0)
    return pl.pallas_call(
        functools.partial(_ln1_kernel, alpha=alpha),
        grid=(g // tg, nt),
        in_specs=[pl.BlockSpec((tg, tt, d), lambda a, b: (a, b, 0)), pl.BlockSpec((tg * tt, d), rows),
                  pl.BlockSpec((tg, 1, d), grp), pl.BlockSpec((tg, 1, d), grp), pl.BlockSpec((tg, 1, d), grp),
                  pl.BlockSpec((1, d), vec), pl.BlockSpec((1, d), vec)],
        out_specs=[pl.BlockSpec((tg * tt, d), rows)] * 2,
        out_shape=[jax.ShapeDtypeStruct((g * t, d), F32), jax.ShapeDtypeStruct((g * t, d), BF16)],
        compiler_params=_params(48, ("arbitrary", "arbitrary")),
        name="ln1_modulate",
    )(x3, y, g1, sc, sh, ln_g, ln_b)


def _topk_rows(vals, k, payload=None):
    n = vals.shape[0]
    iota = lax.broadcasted_iota(jnp.int32, vals.shape, 0).astype(F32)
    out_v, out_i = [], []
    for _ in range(k):
        m = jnp.max(vals, axis=0, keepdims=True)
        am = jnp.min(jnp.where(vals == m, iota, float(n)), axis=0, keepdims=True)
        hit = iota == am
        out_v.append(m)
        out_i.append(am if payload is None else jnp.sum(jnp.where(hit, payload, 0.0), axis=0, keepdims=True))
        vals = jnp.where(hit, -jnp.inf, vals)
    return jnp.concatenate(out_v, axis=0), jnp.concatenate(out_i, axis=0)


def _peer_select_kernel(q_ref, sk_ref, idx_ref, gate_ref):
    tm = q_ref.shape[0]
    idx_rows, gate_rows = [], []
    for h in range(PEER_HEADS):
        tops = []
        for p in range(2):
            c0 = (h * 2 + p) * PEER_HALF
            qhp = q_ref[:, c0:c0 + PEER_HALF].astype(BF16)
            s_t = lax.dot_general(sk_ref[h, p], qhp, _NT, preferred_element_type=F32)
            tops.append(_topk_rows(s_t, PEER_TOPK))
        (v1, i1), (v2, i2) = tops
        nb = [PEER_TOPK // (a + 1) for a in range(PEER_TOPK)]
        pad = (-sum(nb)) % 8
        cand = jnp.concatenate([v1[a:a + 1, :] + v2[:nb[a], :] for a in range(PEER_TOPK)]
                               + [jnp.full((pad, tm), -jnp.inf, F32)], axis=0)
        cidx = jnp.concatenate([i1[a:a + 1, :] * float(N_KEYS) + i2[:nb[a], :] for a in range(PEER_TOPK)]
                               + [jnp.zeros((pad, tm), F32)], axis=0)
        best, eidx = _topk_rows(cand, PEER_TOPK, payload=cidx)
        e = jnp.exp(best - best[0:1, :])
        gate_rows.append(e / e.sum(axis=0, keepdims=True))
        idx_rows.append(eidx)
    idx_ref[...] = jnp.concatenate(idx_rows, axis=0).T.astype(jnp.int32)
    gate_ref[...] = jnp.concatenate(gate_rows, axis=0).T


def peer_select(qp, subkeys_bf16, tm=128):
    rows, w = qp.shape
    tm = min(tm, rows)
    blk = lambda i: (i, 0)
    return pl.pallas_call(
        _peer_select_kernel,
        grid=(rows // tm,),
        in_specs=[pl.BlockSpec((tm, w), blk), pl.BlockSpec(subkeys_bf16.shape, lambda i: (0, 0, 0, 0))],
        out_specs=[pl.BlockSpec((tm, PEER_K), blk), pl.BlockSpec((tm, PEER_K), blk)],
        out_shape=[jax.ShapeDtypeStruct((rows, PEER_K), jnp.int32), jax.ShapeDtypeStruct((rows, PEER_K), F32)],
        compiler_params=_params(32, ("arbitrary",)),
        name="peer_select",
    )(qp, subkeys_bf16)


def _pack_kernel(u_ref, v_ref, o_ref):
    half = u_ref.shape[1] // 2
    o_ref[:, 0, :half] = u_ref[:, :half].astype(BF16)
    o_ref[:, 0, half:] = v_ref[:, :half].astype(BF16)
    o_ref[:, 1, :half] = u_ref[:, half:].astype(BF16)
    o_ref[:, 1, half:] = v_ref[:, half:].astype(BF16)


def pack_expert_tables(peer_u, peer_v, tr=256):
    e, d = peer_u.shape
    blk = pl.BlockSpec((tr, d), lambda i: (i, 0))
    return pl.pallas_call(
        _pack_kernel,
        grid=(e // tr,),
        in_specs=[blk, blk],
        out_specs=pl.BlockSpec((tr, 2, d), lambda i: (i, 0, 0)),
        out_shape=jax.ShapeDtypeStruct((e, 2, d), BF16),
        compiler_params=_params(48, ("arbitrary",)),
        name="pack_tables",
    )(peer_u, peer_v)


def _peer_mix_kernel(idx_ref, nidx_ref, gate_ref, tab_ref, x1_ref, sc_ref, sh_ref, g2_ref, lg_ref, lb_ref, o_ref,
                     buf_a, buf_b, sem_a, sem_b, x_ref, y_ref, *, tt, nsteps, alpha):
    i = pl.program_id(0)
    half = x_ref.shape[1] // 2
    nchunk = half // LANES
    x_ref[...] = (x1_ref[...] * (1.0 + sc_ref[...]) + sh_ref[...]).reshape(x_ref.shape)

    def issue(ids, row, buf, sem, t):
        for k in range(PEER_K):
            pltpu.make_async_copy(tab_ref.at[ids[row, k]], buf.at[t, pl.ds(2 * k, 2), :], sem.at[t]).start()

    def wait(buf, sem, t):
        pltpu.make_async_copy(buf.at[t], buf.at[t], sem.at[t]).wait()

    @pl.when(i == 0)
    def _():
        def first(t, c):
            issue(idx_ref, t, buf_a, sem_a, t)
            return c
        lax.fori_loop(0, tt, first, 0)

    kr = 2 * PEER_K
    sub = lax.broadcasted_iota(jnp.int32, (kr, LANES), 0)
    even = sub % 2 == 0
    pair_eye = sub // 2 == lax.broadcasted_iota(jnp.int32, (kr, LANES), 1)
    row_id = lax.broadcasted_iota(jnp.int32, (tt, LANES), 0)
    y_ref[...] = jnp.zeros(y_ref.shape, F32)

    def mix(buf, t, row0):
        x = x_ref[pl.ds(row0 + t, 1), :]
        acc = jnp.zeros((kr // 8, 8, LANES), F32)
        for ch in range(nchunk):
            cs = slice(ch * LANES, (ch + 1) * LANES)
            xs = jnp.where(even[:8], x[:, cs], x[:, half + ch * LANES:half + (ch + 1) * LANES])
            acc = acc + buf[t, :, cs].astype(F32).reshape(kr // 8, 8, LANES) * xs[None]
        acc = acc.reshape(kr, LANES)
        acc = acc + jnp.where(even, pltpu.roll(acc, kr - 1, 0), pltpu.roll(acc, 1, 0))
        a = acc.sum(axis=1, keepdims=True)
        gate = jnp.sum(jnp.where(pair_eye, gate_ref[pl.ds(row0 + t, 1), :], 0.0), axis=1, keepdims=True)
        act = 0.5 * a * (1.0 + lax.erf(a * (2.0 ** -0.5))) * gate
        act_b = jnp.broadcast_to(act, (kr, LANES))
        mine = row_id == t
        rows = slice(row0, row0 + tt)
        for ch in range(nchunk):
            cs = slice(half + ch * LANES, half + (ch + 1) * LANES)
            s = (act_b * buf[t, :, cs].astype(F32)).reshape(kr // 8, 8, LANES).sum(axis=0)
            s = s + pltpu.roll(s, 2, 0)
            s = s + pltpu.roll(s, 4, 0)
            for r, c0 in ((0, ch * LANES), (1, half + ch * LANES)):
                ys = slice(c0, c0 + LANES)
                y_ref[rows, ys] = jnp.where(mine, s[r:r + 1, :], y_ref[rows, ys])

    def token_a(t, c):
        wait(buf_a, sem_a, t)
        issue(idx_ref, tt + t, buf_b, sem_b, t)
        mix(buf_a, t, 0)
        return c

    def token_b(t, c):
        wait(buf_b, sem_b, t)
        issue(nidx_ref, t, buf_a, sem_a, t)
        mix(buf_b, t, tt)
        return c

    lax.fori_loop(0, tt, token_a, 0)
    lax.fori_loop(0, tt, token_b, 0)

    @pl.when(i == nsteps - 1)
    def _():
        for t in range(tt):
            wait(buf_a, sem_a, t)

    z = alpha * x1_ref[...] + g2_ref[...] * y_ref[...].reshape(x1_ref.shape)
    o_ref[...] = _layer_norm(z, lg_ref[...], lb_ref[...])


def peer_mix_ln2(eidx, gate, table, x1_3, sc, sh, g2, ln_g, ln_b, alpha, tt=8):
    g, t, d = x1_3.shape
    rows = g * t
    tb = 2 * tt
    nsteps = rows // tb
    bt = min(tb, t)
    bg = tb // bt
    nt = t // bt
    blk = lambda i: (i, 0)
    nxt = lambda i: ((i + 1) % nsteps, 0)
    tok3 = pl.BlockSpec((bg, bt, d), lambda i: (i // nt, i % nt, 0))
    grp = pl.BlockSpec((bg, 1, d), lambda i: (i // nt, 0, 0))
    vec = pl.BlockSpec((1, d), lambda i: (0, 0))
    return pl.pallas_call(
        functools.partial(_peer_mix_kernel, tt=tt, nsteps=nsteps, alpha=alpha),
        grid=(nsteps,),
        in_specs=[pl.BlockSpec((tb, PEER_K), blk, memory_space=pltpu.SMEM),
                  pl.BlockSpec((tb, PEER_K), nxt, memory_space=pltpu.SMEM),
                  pl.BlockSpec((tb, PEER_K), blk),
                  pl.BlockSpec(memory_space=pl.ANY),
                  tok3, grp, grp, grp, vec, vec],
        out_specs=tok3,
        out_shape=jax.ShapeDtypeStruct((g, t, d), F32),
        scratch_shapes=[pltpu.VMEM((tt, 2 * PEER_K, d), BF16), pltpu.VMEM((tt, 2 * PEER_K, d), BF16),
                        pltpu.SemaphoreType.DMA((tt,)), pltpu.SemaphoreType.DMA((tt,)),
                        pltpu.VMEM((tb, d), F32), pltpu.VMEM((tb, d), F32)],
        compiler_params=_params(56, ("arbitrary",)),
        name="peer_mix",
    )(eidx, eidx, gate, table, x1_3, sc, sh, g2, ln_g, ln_b)


def _layer(x3, mods, lw, pos0, tg, tt, prev=None):
    g, t, d = x3.shape
    sh1, sc1, g1, sh2, sc2, g2 = mods
    aw = N_HEADS * HEAD_DIM
    alpha = lw["alpha"]

    h = modulate(x3, sc1, sh1, tg, tt)
    q = matmul([(h, lw["w_in"], 0, 0)], aw, name="in_q")
    kv = matmul([(h, lw["w_in"], 0, aw)], 2 * KV_WIDTH, name="in_kv")
    u = matmul([(h, lw["w_in"], 0, aw + 2 * KV_WIDTH)], lw["w_in"].shape[1] - aw - 2 * KV_WIDTH, name="in_u")

    tabs = rope_tables(pos0 + jnp.arange(t, dtype=jnp.int32))
    if prev is None:
        attn, krot = attn_prompt(q, kv, tabs, lw["sinks"], g, t)
        state16 = jnp.zeros((g, POOL_HALO, u.shape[1]), F32)
        new_k = krot.reshape(g, t, KV_WIDTH)[:, -WINDOW:].reshape(g, WINDOW, N_KV_HEADS, HEAD_DIM)
        new_v = kv.reshape(g, t, 2 * KV_WIDTH)[:, -WINDOW:, KV_WIDTH:].reshape(g, WINDOW, N_KV_HEADS, HEAD_DIM)
    else:
        cache_k, cache_v, state = prev
        wc = cache_k.shape[1]
        attn, krot = attn_sample(q, kv, cache_k.reshape(g, wc, KV_WIDTH), cache_v.reshape(g, wc, KV_WIDTH),
                                 tabs, lw["sinks"], t)
        state16 = jnp.pad(state, ((0, 0), (POOL_HALO - POOL_STATE, 0), (0, 0)))
        new_k = jnp.concatenate([cache_k, krot.reshape(g, t, N_KV_HEADS, HEAD_DIM)], axis=1)[:, -wc:]
        new_v = jnp.concatenate([cache_v, kv[:, KV_WIDTH:].reshape(g, t, N_KV_HEADS, HEAD_DIM)], axis=1)[:, -wc:]
    u3 = u.reshape(g, t, u.shape[1])
    new_pool = jnp.concatenate([state16, u3], axis=1)[:, -POOL_STATE:]
    pooled = pool_mix(u3, state16, lw["pool_w"], lw["pool_scale"], min(tg, POOL_TG), tt, pos0)

    y1 = matmul([(attn, lw["w_out"], 0, 0), (pooled, lw["w_out"], 1, 0)], d, name="out_proj")
    ln_tt = min(t, LN_ROWS)
    ln_tg = LN_ROWS // ln_tt
    x1, h2b = ln1_modulate(x3, y1, g1, sc2, sh2, lw["ln1_g"], lw["ln1_b"], ln_tg, ln_tt, alpha)

    qp = matmul([(h2b, lw["peer_wq"], 0, 0)], lw["peer_wq"].shape[1], name="peer_query")
    eidx, gate = peer_select(qp, lw["subkeys"])
    out = peer_mix_ln2(eidx, gate, lw["table"], x1.reshape(g, t, d), sc2, sh2, g2, lw["ln2_g"], lw["ln2_b"], alpha)
    return out, (new_k, new_v, new_pool)


def kernel(x_prompt, x_sample, cache_k, cache_v, state_pool, c_prompt, c_sample, w_ada, b_ada, w_in, sinks,
           pool_w, pool_scale, w_out, ln1_g, ln1_b, peer_wq, peer_subkeys, peer_u, peer_v, ln2_g, ln2_b):
    depth = w_ada.shape[0]
    bp, seq, d = x_prompt.shape
    bs, t_new, _ = x_sample.shape
    past_len = PAST_LEN
    alpha = (2 * depth) ** 0.25
    aw = N_HEADS * HEAD_DIM
    n_c = bp + bs
    pad = (-n_c) % 8

    xp, xs = x_prompt, x_sample
    outs = [[] for _ in range(6)]
    for l in range(depth):
        c_all = jnp.concatenate([c_prompt, c_sample, jnp.zeros((pad, d), F32)], axis=0)
        mod = adaln(c_all, w_ada[l], b_ada[l][None, :])
        mod_p = mod[:bp].reshape(bp, N_MOD, 1, d)
        mod_s = mod[bp:n_c].reshape(bs, N_MOD, 1, d)
        mods_p = tuple(mod_p[:, i] for i in range(N_MOD))
        mods_s = tuple(mod_s[:, i] for i in range(N_MOD))
        lw = dict(
            alpha=alpha, w_in=w_in[l], w_out=w_out[l], peer_wq=peer_wq[l],
            sinks=sinks[l], pool_w=pool_w[l].astype(BF16), pool_scale=pool_scale[l][None, :],
            ln1_g=ln1_g[l][None, :], ln1_b=ln1_b[l][None, :], ln2_g=ln2_g[l][None, :], ln2_b=ln2_b[l][None, :],
            subkeys=peer_subkeys[l].astype(BF16),
            table=pack_expert_tables(peer_u[l], peer_v[l]),
        )
        xp, (k1, v1, p1) = _layer(xp, mods_p, lw, 0, 1, 512)
        xs, (k2, v2, p2) = _layer(xs, mods_s, lw, past_len, 64, t_new, prev=(cache_k[l], cache_v[l], state_pool[l]))
        for lst, val in zip(outs, (k1, v1, p1, k2, v2, p2)):
            lst.append(val)
    return (xp, xs) + tuple(jnp.stack(o) for o in outs)
```

```python
import functools

import jax
import jax.numpy as jnp
from jax import lax
from jax.experimental import pallas as pl
from jax.experimental.pallas import tpu as pltpu

F32 = jnp.float32
BF16 = jnp.bfloat16

LANES = 128
HEAD_DIM = 64
N_HEADS = 32
N_KV_HEADS = 4
GROUP = N_HEADS // N_KV_HEADS
KV_WIDTH = N_KV_HEADS * HEAD_DIM
WINDOW = 128
ROT_DIM = HEAD_DIM // 4
ROT_HALF = ROT_DIM // 2
ROPE_THETA = 500000.0
POOL_WINDOWS = (2, 4, 8, 16)
POOL_GROUP_WIDTH = 512
POOL_HALO = 16
POOL_STATE = 15
PEER_HEADS = 8
PEER_HALF = 128
N_KEYS = 128
PEER_TOPK = 16
PEER_K = PEER_HEADS * PEER_TOPK
N_MOD = 6
PAST_LEN = 8192
LN_ROWS = 128
POOL_TG = 16
LN_EPS = 1e-5
NEG_INF = -1e30


def _params(vmem_mb, sem=None):
    return pltpu.CompilerParams(dimension_semantics=sem, vmem_limit_bytes=vmem_mb << 20)


def _ada_kernel(c_ref, w_ref, b_ref, o_ref):
    c = c_ref[...]
    s = (c * jax.nn.sigmoid(c)).astype(BF16)
    o_ref[...] = jnp.dot(s, w_ref[...].astype(BF16), preferred_element_type=F32) + b_ref[...]


def adaln(c_all, w_ada, b_ada, tn=512):
    r, d = c_all.shape
    n = w_ada.shape[1]
    return pl.pallas_call(
        _ada_kernel,
        grid=(n // tn,),
        in_specs=[pl.BlockSpec((r, d), lambda j: (0, 0)),
                  pl.BlockSpec((d, tn), lambda j: (0, j)),
                  pl.BlockSpec((1, tn), lambda j: (0, j))],
        out_specs=pl.BlockSpec((r, tn), lambda j: (0, j)),
        out_shape=jax.ShapeDtypeStruct((r, n), F32),
        compiler_params=_params(48, ("arbitrary",)),
        name="adaln",
    )(c_all, w_ada, b_ada)


def _mod_kernel(x_ref, sc_ref, sh_ref, o_ref):
    h = x_ref[...] * (1.0 + sc_ref[...]) + sh_ref[...]
    o_ref[...] = h.reshape(o_ref.shape).astype(o_ref.dtype)


def modulate(x3, sc, sh, tg, tt):
    g, t, d = x3.shape
    nt = t // tt
    return pl.pallas_call(
        _mod_kernel,
        grid=(g // tg, nt),
        in_specs=[pl.BlockSpec((tg, tt, d), lambda a, b: (a, b, 0)),
                  pl.BlockSpec((tg, 1, d), lambda a, b: (a, 0, 0)),
                  pl.BlockSpec((tg, 1, d), lambda a, b: (a, 0, 0))],
        out_specs=pl.BlockSpec((tg * tt, d), lambda a, b: (a * nt + b, 0)),
        out_shape=jax.ShapeDtypeStruct((g * t, d), BF16),
        compiler_params=_params(48, ("arbitrary", "arbitrary")),
        name="modulate",
    )(x3, sc, sh)


def _mm_kernel(*refs, n_pairs):
    o_ref = refs[-1]
    acc = None
    for p in range(n_pairs):
        part = jnp.dot(refs[2 * p][...], refs[2 * p + 1][...].astype(BF16), preferred_element_type=F32)
        acc = part if acc is None else acc + part
    o_ref[...] = acc.astype(o_ref.dtype)


def matmul(pairs, n, tm=1024, tn=512, out_dtype=F32, name="matmul"):
    m = pairs[0][0].shape[0]
    tm, tn = min(tm, m), min(tn, n)
    in_specs, args = [], []
    for x, w, r, c in pairs:
        k = x.shape[1]
        assert c % tn == 0 and w.shape[0] % k == 0
        cb = c // tn
        in_specs += [pl.BlockSpec((tm, k), lambda i, j: (i, 0)),
                     pl.BlockSpec((k, tn), lambda i, j, r=r, cb=cb: (r, cb + j))]
        args += [x, w]
    return pl.pallas_call(
        functools.partial(_mm_kernel, n_pairs=len(pairs)),
        grid=(m // tm, n // tn),
        in_specs=in_specs,
        out_specs=pl.BlockSpec((tm, tn), lambda i, j: (i, j)),
        out_shape=jax.ShapeDtypeStruct((m, n), out_dtype),
        compiler_params=_params(48, ("arbitrary", "arbitrary")),
        name=name,
    )(*args)


def rope_tables(pos):
    inv_freq = ROPE_THETA ** (-jnp.arange(ROT_HALF, dtype=F32) / ROT_HALF)
    ang = pos.astype(F32)[:, None] * inv_freq[None, :]
    cos, sin = jnp.cos(ang), jnp.sin(ang)
    t = pos.shape[0]
    ones = jnp.ones((t, HEAD_DIM - ROT_DIM), F32)
    zeros_h = jnp.zeros((t, ROT_HALF), F32)
    zeros_r = jnp.zeros((t, HEAD_DIM - ROT_DIM), F32)
    c = jnp.concatenate([cos, cos, ones], axis=1)
    s1 = jnp.concatenate([zeros_h, sin, zeros_r], axis=1)
    s2 = jnp.concatenate([-sin, zeros_h, zeros_r], axis=1)
    rep = LANES // HEAD_DIM
    return tuple(jnp.tile(a, (1, rep)) for a in (c, s1, s2))


def _rope(x, c, s1, s2):
    rows, w = x.shape
    reps = (rows // c.shape[0], w // LANES)
    ct, s1t, s2t = (jnp.tile(a, reps) for a in (c, s1, s2))
    return x * ct + pltpu.roll(x, ROT_HALF, 1) * s1t + pltpu.roll(x, w - ROT_HALF, 1) * s2t


def _sink_softmax_pv(pieces, sink):
    m = sink
    for s, _ in pieces:
        m = jnp.maximum(m, s.max(-1, keepdims=True))
    denom = jnp.exp(sink - m)
    o = None
    for s, v in pieces:
        p = jnp.exp(s - m)
        denom = denom + p.sum(-1, keepdims=True)
        pv = jnp.dot(p.astype(BF16), v, preferred_element_type=F32)
        o = pv if o is None else o + pv
    return o / denom


_NT = (((1,), (1,)), ((), ()))


def _attn_prompt_kernel(q_ref, kp_ref, kc_ref, vp_ref, vc_ref, cc_ref, s1c_ref, s2c_ref,
                        cp_ref, s1p_ref, s2p_ref, sink_ref, o_ref, krot_ref):
    i = pl.program_id(1)
    q = (_rope(q_ref[...], cc_ref[...], s1c_ref[...], s2c_ref[...]) * (HEAD_DIM ** -0.5)).astype(BF16)
    kc = _rope(kc_ref[...], cc_ref[...], s1c_ref[...], s2c_ref[...])
    krot_ref[...] = kc
    kp = _rope(kp_ref[...], cp_ref[...], s1p_ref[...], s2p_ref[...])
    kc, kp = kc.astype(BF16), kp.astype(BF16)
    vc, vp = vc_ref[...].astype(BF16), vp_ref[...].astype(BF16)
    gr = GROUP * WINDOW
    r = lax.broadcasted_iota(jnp.int32, (gr, WINDOW), 0) % WINDOW
    j = lax.broadcasted_iota(jnp.int32, (gr, WINDOW), 1)
    mask_c = j <= r
    mask_p = (j >= r) & (i > 0)
    outs = []
    for h in range(N_KV_HEADS):
        hs = slice(h * HEAD_DIM, (h + 1) * HEAD_DIM)
        qg = jnp.concatenate([q[:, (h * GROUP + g) * HEAD_DIM:(h * GROUP + g + 1) * HEAD_DIM]
                              for g in range(GROUP)], axis=0)
        sink = jnp.concatenate([jnp.full((WINDOW, 1), sink_ref[h * GROUP + g], F32) for g in range(GROUP)], axis=0)
        sp = jnp.where(mask_p, lax.dot_general(qg, kp[:, hs], _NT, preferred_element_type=F32), NEG_INF)
        sc = jnp.where(mask_c, lax.dot_general(qg, kc[:, hs], _NT, preferred_element_type=F32), NEG_INF)
        o = _sink_softmax_pv([(sp, vp[:, hs]), (sc, vc[:, hs])], sink)
        outs += [o[g * WINDOW:(g + 1) * WINDOW, :] for g in range(GROUP)]
    o_ref[...] = jnp.concatenate(outs, axis=1).astype(o_ref.dtype)


def attn_prompt(q, kv, tabs, sinks, batch, seq):
    nb = seq // WINDOW
    aw = N_HEADS * HEAD_DIM
    cur = lambda b, i: (b * nb + i, 0)
    prev = lambda b, i: (b * nb + jnp.maximum(i - 1, 0), 0)
    cur_v = lambda b, i: (b * nb + i, 1)
    prev_v = lambda b, i: (b * nb + jnp.maximum(i - 1, 0), 1)
    tcur = lambda b, i: (i, 0)
    tprev = lambda b, i: (jnp.maximum(i - 1, 0), 0)
    tspec = lambda f: pl.BlockSpec((WINDOW, LANES), f)
    return pl.pallas_call(
        _attn_prompt_kernel,
        grid=(batch, nb),
        in_specs=[pl.BlockSpec((WINDOW, aw), cur),
                  pl.BlockSpec((WINDOW, KV_WIDTH), prev), pl.BlockSpec((WINDOW, KV_WIDTH), cur),
                  pl.BlockSpec((WINDOW, KV_WIDTH), prev_v), pl.BlockSpec((WINDOW, KV_WIDTH), cur_v),
                  tspec(tcur), tspec(tcur), tspec(tcur), tspec(tprev), tspec(tprev), tspec(tprev),
                  pl.BlockSpec(memory_space=pltpu.SMEM)],
        out_specs=[pl.BlockSpec((WINDOW, aw), cur), pl.BlockSpec((WINDOW, KV_WIDTH), cur)],
        out_shape=[jax.ShapeDtypeStruct((batch * seq, aw), BF16),
                   jax.ShapeDtypeStruct((batch * seq, KV_WIDTH), F32)],
        compiler_params=_params(32, ("arbitrary", "arbitrary")),
        name="attn_prompt",
    )(q, kv, kv, kv, kv, *tabs, *tabs, sinks)


def _attn_sample_kernel(q_ref, kv_ref, ck_ref, cv_ref, c_ref, s1_ref, s2_ref, sink_ref, o_ref, krot_ref,
                        qs_ref, os_ref, *, nb, t_new):
    tabs = (c_ref[...], s1_ref[...], s2_ref[...])
    qs_ref[...] = (_rope(q_ref[...], *tabs) * (HEAD_DIM ** -0.5)).astype(BF16).astype(F32)
    krot_ref[...] = _rope(kv_ref[:, :KV_WIDTH], *tabs)
    wc = ck_ref.shape[1]
    gr = GROUP * t_new
    tok = lax.broadcasted_iota(jnp.int32, (gr, wc), 0) % t_new
    diff_c = tok + wc - lax.broadcasted_iota(jnp.int32, (gr, wc), 1)
    mask_c = (diff_c >= 0) & (diff_c <= WINDOW)
    mask_n = (lax.broadcasted_iota(jnp.int32, (gr, t_new), 1)
              <= lax.broadcasted_iota(jnp.int32, (gr, t_new), 0) % t_new)
    sink_cols = [jnp.concatenate([jnp.full((t_new, 1), sink_ref[h * GROUP + g], F32) for g in range(GROUP)], axis=0)
                 for h in range(N_KV_HEADS)]

    def one(n):
        rows = pl.ds(pl.multiple_of(n * t_new, t_new), t_new)
        q = qs_ref[rows, :]
        kn = krot_ref[rows, :].astype(BF16)
        vn = kv_ref[rows, KV_WIDTH:].astype(BF16)
        kc = ck_ref[n].astype(BF16)
        vc = cv_ref[n].astype(BF16)
        outs = []
        for h in range(N_KV_HEADS):
            hs = slice(h * HEAD_DIM, (h + 1) * HEAD_DIM)
            qg = jnp.concatenate([q[:, (h * GROUP + g) * HEAD_DIM:(h * GROUP + g + 1) * HEAD_DIM]
                                  for g in range(GROUP)], axis=0).astype(BF16)
            sc = jnp.where(mask_c, lax.dot_general(qg, kc[:, hs], _NT, preferred_element_type=F32), NEG_INF)
            sn = jnp.where(mask_n, lax.dot_general(qg, kn[:, hs], _NT, preferred_element_type=F32), NEG_INF)
            o = _sink_softmax_pv([(sc, vc[:, hs]), (sn, vn[:, hs])], sink_cols[h])
            outs += [o[g * t_new:(g + 1) * t_new, :] for g in range(GROUP)]
        os_ref[rows, :] = jnp.concatenate(outs, axis=1)

    def body(n2, carry):
        one(2 * n2)
        one(2 * n2 + 1)
        return carry

    lax.fori_loop(0, nb // 2, body, 0)
    o_ref[...] = os_ref[...].astype(o_ref.dtype)


def attn_sample(q, kv, cache_k, cache_v, tabs, sinks, t_new, nb=8):
    rows = q.shape[0]
    n = rows // t_new
    wc = cache_k.shape[1]
    aw = N_HEADS * HEAD_DIM
    tm = nb * t_new
    blk = lambda i: (i, 0)
    full = lambda i: (0, 0)
    return pl.pallas_call(
        functools.partial(_attn_sample_kernel, nb=nb, t_new=t_new),
        grid=(n // nb,),
        in_specs=[pl.BlockSpec((tm, aw), blk), pl.BlockSpec((tm, 2 * KV_WIDTH), blk),
                  pl.BlockSpec((nb, wc, KV_WIDTH), lambda i: (i, 0, 0)),
                  pl.BlockSpec((nb, wc, KV_WIDTH), lambda i: (i, 0, 0)),
                  pl.BlockSpec((t_new, LANES), full), pl.BlockSpec((t_new, LANES), full),
                  pl.BlockSpec((t_new, LANES), full),
                  pl.BlockSpec(memory_space=pltpu.SMEM)],
        out_specs=[pl.BlockSpec((tm, aw), blk), pl.BlockSpec((tm, KV_WIDTH), blk)],
        out_shape=[jax.ShapeDtypeStruct((rows, aw), BF16), jax.ShapeDtypeStruct((rows, KV_WIDTH), F32)],
        scratch_shapes=[pltpu.VMEM((tm, aw), F32), pltpu.VMEM((tm, aw), F32)],
        compiler_params=_params(32, ("arbitrary",)),
        name="attn_sample",
    )(q, kv, cache_k, cache_v, *tabs, sinks)


def _pool_kernel(u_ref, halo_ref, st_ref, pw_ref, scale_ref, o_ref, ext_ref, *, pos0):
    ti = pl.program_id(1)
    tg, tt, _ = u_ref.shape
    ext_ref[:, 0:POOL_HALO, :] = jnp.where(ti == 0, st_ref[...], halo_ref[...])
    ext_ref[:, POOL_HALO:, :] = u_ref[...]
    pos = pos0 + ti * tt + lax.broadcasted_iota(jnp.int32, (1, tt, 1), 1)
    for g, w in enumerate(POOL_WINDOWS):
        cols = slice(g * POOL_GROUP_WIDTH, (g + 1) * POOL_GROUP_WIDTH)
        acc = ext_ref[:, POOL_HALO:POOL_HALO + tt, cols]
        for k in range(1, w):
            acc = acc + ext_ref[:, POOL_HALO - k:POOL_HALO - k + tt, cols]
        inv_cnt = 1.0 / jnp.minimum(w, pos + 1).astype(F32)
        d = acc * inv_cnt - u_ref[:, :, cols]
        d = d.reshape(tg * tt, POOL_GROUP_WIDTH).astype(BF16)
        y = jnp.dot(d, pw_ref[g], preferred_element_type=F32) * scale_ref[:, cols]
        o_ref[:, cols] = y.astype(o_ref.dtype)


def pool_mix(u3, state16, pool_w, pool_scale, tg, tt, pos0):
    g, t, pw = u3.shape
    nt = t // tt
    hb = tt // POOL_HALO
    halo_src = u3 if nt > 1 else state16
    return pl.pallas_call(
        functools.partial(_pool_kernel, pos0=pos0),
        grid=(g // tg, nt),
        in_specs=[pl.BlockSpec((tg, tt, pw), lambda a, b: (a, b, 0)),
                  pl.BlockSpec((tg, POOL_HALO, pw), lambda a, b: (a, jnp.maximum(b * hb - 1, 0), 0)),
                  pl.BlockSpec((tg, POOL_HALO, pw), lambda a, b: (a, 0, 0)),
                  pl.BlockSpec(pool_w.shape, lambda a, b: (0, 0, 0)),
                  pl.BlockSpec((1, pw), lambda a, b: (0, 0))],
        out_specs=pl.BlockSpec((tg * tt, pw), lambda a, b: (a * nt + b, 0)),
        out_shape=jax.ShapeDtypeStruct((g * t, pw), BF16),
        scratch_shapes=[pltpu.VMEM((tg, POOL_HALO + tt, pw), F32)],
        compiler_params=_params(48, ("arbitrary", "arbitrary")),
        name="pool_mix",
    )(u3, halo_src, state16, pool_w, pool_scale)


def _layer_norm(z, g, b):
    mu = z.mean(-1, keepdims=True)
    zc = z - mu
    var = (zc * zc).mean(-1, keepdims=True)
    return zc * lax.rsqrt(var + LN_EPS) * g + b


def _ln1_kernel(x_ref, y_ref, g1_ref, sc_ref, sh_ref, lg_ref, lb_ref, x1_ref, hb_ref, *, alpha):
    x = x_ref[...]
    z = alpha * x + g1_ref[...] * y_ref[...].reshape(x.shape)
    x1 = _layer_norm(z, lg_ref[...], lb_ref[...])
    h = x1 * (1.0 + sc_ref[...]) + sh_ref[...]
    x1_ref[...] = x1.reshape(x1_ref.shape)
    hb_ref[...] = h.reshape(hb_ref.shape).astype(BF16)


def ln1_modulate(x3, y, g1, sc, sh, ln_g, ln_b, tg, tt, alpha):
    g, t, d = x3.shape
    nt = t // tt
    rows = lambda a, b: (a * nt + b, 0)
    grp = lambda a, b: (a, 0, 0)
    vec = lambda a, b: (0, 0)
    return pl.pallas_call(
        functools.partial(_ln1_kernel, alpha=alpha),
        grid=(g // tg, nt),
        in_specs=[pl.BlockSpec((tg, tt, d), lambda a, b: (a, b, 0)), pl.BlockSpec((tg * tt, d), rows),
                  pl.BlockSpec((tg, 1, d), grp), pl.BlockSpec((tg, 1, d), grp), pl.BlockSpec((tg, 1, d), grp),
                  pl.BlockSpec((1, d), vec), pl.BlockSpec((1, d), vec)],
        out_specs=[pl.BlockSpec((tg * tt, d), rows)] * 2,
        out_shape=[jax.ShapeDtypeStruct((g * t, d), F32), jax.ShapeDtypeStruct((g * t, d), BF16)],
        compiler_params=_params(48, ("arbitrary", "arbitrary")),
        name="ln1_modulate",
    )(x3, y, g1, sc, sh, ln_g, ln_b)


def _topk_rows(vals, k, payload=None):
    n = vals.shape[0]
    iota = lax.broadcasted_iota(jnp.int32, vals.shape, 0).astype(F32)
    out_v, out_i = [], []
    for _ in range(k):
        m = jnp.max(vals, axis=0, keepdims=True)
        am = jnp.min(jnp.where(vals == m, iota, float(n)), axis=0, keepdims=True)
        hit = iota == am
        out_v.append(m)
        out_i.append(am if payload is None else jnp.sum(jnp.where(hit, payload, 0.0), axis=0, keepdims=True))
        vals = jnp.where(hit, -jnp.inf, vals)
    return jnp.concatenate(out_v, axis=0), jnp.concatenate(out_i, axis=0)


def _peer_select_kernel(q_ref, sk_ref, idx_ref, gate_ref):
    tm = q_ref.shape[0]
    idx_rows, gate_rows = [], []
    for h in range(PEER_HEADS):
        tops = []
        for p in range(2):
            c0 = (h * 2 + p) * PEER_HALF
            qhp = q_ref[:, c0:c0 + PEER_HALF].astype(BF16)
            s_t = lax.dot_general(sk_ref[h, p], qhp, _NT, preferred_element_type=F32)
            tops.append(_topk_rows(s_t, PEER_TOPK))
        (v1, i1), (v2, i2) = tops
        nb = [PEER_TOPK // (a + 1) for a in range(PEER_TOPK)]
        pad = (-sum(nb)) % 8
        cand = jnp.concatenate([v1[a:a + 1, :] + v2[:nb[a], :] for a in range(PEER_TOPK)]
                               + [jnp.full((pad, tm), -jnp.inf, F32)], axis=0)
        cidx = jnp.concatenate([i1[a:a + 1, :] * float(N_KEYS) + i2[:nb[a], :] for a in range(PEER_TOPK)]
                               + [jnp.zeros((pad, tm), F32)], axis=0)
        best, eidx = _topk_rows(cand, PEER_TOPK, payload=cidx)
        e = jnp.exp(best - best[0:1, :])
        gate_rows.append(e / e.sum(axis=0, keepdims=True))
        idx_rows.append(eidx)
    idx_ref[...] = jnp.concatenate(idx_rows, axis=0).T.astype(jnp.int32)
    gate_ref[...] = jnp.concatenate(gate_rows, axis=0).T


def peer_select(qp, subkeys_bf16, tm=128):
    rows, w = qp.shape
    tm = min(tm, rows)
    blk = lambda i: (i, 0)
    return pl.pallas_call(
        _peer_select_kernel,
        grid=(rows // tm,),
        in_specs=[pl.BlockSpec((tm, w), blk), pl.BlockSpec(subkeys_bf16.shape, lambda i: (0, 0, 0, 0))],
        out_specs=[pl.BlockSpec((tm, PEER_K), blk), pl.BlockSpec((tm, PEER_K), blk)],
        out_shape=[jax.ShapeDtypeStruct((rows, PEER_K), jnp.int32), jax.ShapeDtypeStruct((rows, PEER_K), F32)],
        compiler_params=_params(32, ("arbitrary",)),
        name="peer_select",
    )(qp, subkeys_bf16)


def _pack_kernel(u_ref, v_ref, o_ref):
    half = u_ref.shape[1] // 2
    o_ref[:, 0, :half] = u_ref[:, :half].astype(BF16)
    o_ref[:, 0, half:] = v_ref[:, :half].astype(BF16)
    o_ref[:, 1, :half] = u_ref[:, half:].astype(BF16)
    o_ref[:, 1, half:] = v_ref[:, half:].astype(BF16)


def _select_pack_kernel(q_ref, sk_ref, u_ref, v_ref, idx_ref, gate_ref, tab_ref):
    _peer_select_kernel(q_ref, sk_ref, idx_ref, gate_ref)
    _pack_kernel(u_ref, v_ref, tab_ref)


def peer_select_pack(qp, subkeys_bf16, peer_u, peer_v, tm=128):
    rows, w = qp.shape
    tm = min(tm, rows)
    steps = rows // tm
    e, d = peer_u.shape
    assert e % steps == 0
    tr = e // steps
    blk = lambda i: (i, 0)
    return pl.pallas_call(
        _select_pack_kernel,
        grid=(steps,),
        in_specs=[pl.BlockSpec((tm, w), blk), pl.BlockSpec(subkeys_bf16.shape, lambda i: (0, 0, 0, 0)),
                  pl.BlockSpec((tr, d), blk), pl.BlockSpec((tr, d), blk)],
        out_specs=[pl.BlockSpec((tm, PEER_K), blk), pl.BlockSpec((tm, PEER_K), blk),
                   pl.BlockSpec((tr, 2, d), lambda i: (i, 0, 0))],
        out_shape=[jax.ShapeDtypeStruct((rows, PEER_K), jnp.int32), jax.ShapeDtypeStruct((rows, PEER_K), F32),
                   jax.ShapeDtypeStruct((e, 2, d), BF16)],
        compiler_params=_params(48, ("arbitrary",)),
        name="peer_select_pack",
    )(qp, subkeys_bf16, peer_u, peer_v)


def _peer_mix_kernel(idx_ref, nidx_ref, gate_ref, tab_ref, x1_ref, sc_ref, sh_ref, g2_ref, lg_ref, lb_ref, o_ref,
                     buf_a, buf_b, sem_a, sem_b, x_ref, y_ref, *, tt, nsteps, alpha):
    i = pl.program_id(0)
    half = x_ref.shape[1] // 2
    nchunk = half // LANES
    x_ref[...] = (x1_ref[...] * (1.0 + sc_ref[...]) + sh_ref[...]).reshape(x_ref.shape)

    def issue(ids, row, buf, sem, t):
        for k in range(PEER_K):
            pltpu.make_async_copy(tab_ref.at[ids[row, k]], buf.at[t, pl.ds(2 * k, 2), :], sem.at[t]).start()

    def wait(buf, sem, t):
        pltpu.make_async_copy(buf.at[t], buf.at[t], sem.at[t]).wait()

    @pl.when(i == 0)
    def _():
        def first(t, c):
            issue(idx_ref, t, buf_a, sem_a, t)
            return c
        lax.fori_loop(0, tt, first, 0)

    kr = 2 * PEER_K
    sub = lax.broadcasted_iota(jnp.int32, (kr, LANES), 0)
    even = sub % 2 == 0
    pair_eye = sub // 2 == lax.broadcasted_iota(jnp.int32, (kr, LANES), 1)
    row_id = lax.broadcasted_iota(jnp.int32, (tt, LANES), 0)
    y_ref[...] = jnp.zeros(y_ref.shape, F32)

    def mix(buf, t, row0):
        x = x_ref[pl.ds(row0 + t, 1), :]
        acc = jnp.zeros((kr // 8, 8, LANES), F32)
        for ch in range(nchunk):
            cs = slice(ch * LANES, (ch + 1) * LANES)
            xs = jnp.where(even[:8], x[:, cs], x[:, half + ch * LANES:half + (ch + 1) * LANES])
            acc = acc + buf[t, :, cs].astype(F32).reshape(kr // 8, 8, LANES) * xs[None]
        acc = acc.reshape(kr, LANES)
        acc = acc + jnp.where(even, pltpu.roll(acc, kr - 1, 0), pltpu.roll(acc, 1, 0))
        a = acc.sum(axis=1, keepdims=True)
        gate = jnp.sum(jnp.where(pair_eye, gate_ref[pl.ds(row0 + t, 1), :], 0.0), axis=1, keepdims=True)
        act = 0.5 * a * (1.0 + lax.erf(a * (2.0 ** -0.5))) * gate
        act_b = jnp.broadcast_to(act, (kr, LANES))
        mine = row_id == t
        rows = slice(row0, row0 + tt)
        for ch in range(nchunk):
            cs = slice(half + ch * LANES, half + (ch + 1) * LANES)
            s = (act_b * buf[t, :, cs].astype(F32)).reshape(kr // 8, 8, LANES).sum(axis=0)
            s = s + pltpu.roll(s, 2, 0)
            s = s + pltpu.roll(s, 4, 0)
            for r, c0 in ((0, ch * LANES), (1, half + ch * LANES)):
                ys = slice(c0, c0 + LANES)
                y_ref[rows, ys] = jnp.where(mine, s[r:r + 1, :], y_ref[rows, ys])

    def token_a(t, c):
        wait(buf_a, sem_a, t)
        issue(idx_ref, tt + t, buf_b, sem_b, t)
        mix(buf_a, t, 0)
        return c

    def token_b(t, c):
        wait(buf_b, sem_b, t)
        issue(nidx_ref, t, buf_a, sem_a, t)
        mix(buf_b, t, tt)
        return c

    lax.fori_loop(0, tt, token_a, 0)
    lax.fori_loop(0, tt, token_b, 0)

    @pl.when(i == nsteps - 1)
    def _():
        for t in range(tt):
            wait(buf_a, sem_a, t)

    z = alpha * x1_ref[...] + g2_ref[...] * y_ref[...].reshape(x1_ref.shape)
    o_ref[...] = _layer_norm(z, lg_ref[...], lb_ref[...])


def peer_mix_ln2(eidx, gate, table, x1_3, sc, sh, g2, ln_g, ln_b, alpha, tt=8):
    g, t, d = x1_3.shape
    rows = g * t
    tb = 2 * tt
    nsteps = rows // tb
    bt = min(tb, t)
    bg = tb // bt
    nt = t // bt
    blk = lambda i: (i, 0)
    nxt = lambda i: ((i + 1) % nsteps, 0)
    tok3 = pl.BlockSpec((bg, bt, d), lambda i: (i // nt, i % nt, 0))
    grp = pl.BlockSpec((bg, 1, d), lambda i: (i // nt, 0, 0))
    vec = pl.BlockSpec((1, d), lambda i: (0, 0))
    return pl.pallas_call(
        functools.partial(_peer_mix_kernel, tt=tt, nsteps=nsteps, alpha=alpha),
        grid=(nsteps,),
        in_specs=[pl.BlockSpec((tb, PEER_K), blk, memory_space=pltpu.SMEM),
                  pl.BlockSpec((tb, PEER_K), nxt, memory_space=pltpu.SMEM),
                  pl.BlockSpec((tb, PEER_K), blk),
                  pl.BlockSpec(memory_space=pl.ANY),
                  tok3, grp, grp, grp, vec, vec],
        out_specs=tok3,
        out_shape=jax.ShapeDtypeStruct((g, t, d), F32),
        scratch_shapes=[pltpu.VMEM((tt, 2 * PEER_K, d), BF16), pltpu.VMEM((tt, 2 * PEER_K, d), BF16),
                        pltpu.SemaphoreType.DMA((tt,)), pltpu.SemaphoreType.DMA((tt,)),
                        pltpu.VMEM((tb, d), F32), pltpu.VMEM((tb, d), F32)],
        compiler_params=_params(56, ("arbitrary",)),
        name="peer_mix",
    )(eidx, eidx, gate, table, x1_3, sc, sh, g2, ln_g, ln_b)


def _layer(x3, mods, lw, pos0, tg, tt, prev=None):
    g, t, d = x3.shape
    sh1, sc1, g1, sh2, sc2, g2 = mods
    aw = N_HEADS * HEAD_DIM
    alpha = lw["alpha"]

    h = modulate(x3, sc1, sh1, tg, tt)
    q = matmul([(h, lw["w_in"], 0, 0)], aw, name="in_q")
    kv = matmul([(h, lw["w_in"], 0, aw)], 2 * KV_WIDTH, name="in_kv")
    u = matmul([(h, lw["w_in"], 0, aw + 2 * KV_WIDTH)], lw["w_in"].shape[1] - aw - 2 * KV_WIDTH, name="in_u")

    tabs = rope_tables(pos0 + jnp.arange(t, dtype=jnp.int32))
    if prev is None:
        attn, krot = attn_prompt(q, kv, tabs, lw["sinks"], g, t)
        state16 = jnp.zeros((g, POOL_HALO, u.shape[1]), F32)
        new_k = krot.reshape(g, t, KV_WIDTH)[:, -WINDOW:].reshape(g, WINDOW, N_KV_HEADS, HEAD_DIM)
        new_v = kv.reshape(g, t, 2 * KV_WIDTH)[:, -WINDOW:, KV_WIDTH:].reshape(g, WINDOW, N_KV_HEADS, HEAD_DIM)
    else:
        cache_k, cache_v, state = prev
        wc = cache_k.shape[1]
        attn, krot = attn_sample(q, kv, cache_k.reshape(g, wc, KV_WIDTH), cache_v.reshape(g, wc, KV_WIDTH),
                                 tabs, lw["sinks"], t)
        state16 = jnp.pad(state, ((0, 0), (POOL_HALO - POOL_STATE, 0), (0, 0)))
        new_k = jnp.concatenate([cache_k, krot.reshape(g, t, N_KV_HEADS, HEAD_DIM)], axis=1)[:, -wc:]
        new_v = jnp.concatenate([cache_v, kv[:, KV_WIDTH:].reshape(g, t, N_KV_HEADS, HEAD_DIM)], axis=1)[:, -wc:]
    u3 = u.reshape(g, t, u.shape[1])
    new_pool = jnp.concatenate([state16, u3], axis=1)[:, -POOL_STATE:]
    pooled = pool_mix(u3, state16, lw["pool_w"], lw["pool_scale"], min(tg, POOL_TG), tt, pos0)

    y1 = matmul([(attn, lw["w_out"], 0, 0), (pooled, lw["w_out"], 1, 0)], d, name="out_proj")
    ln_tt = min(t, LN_ROWS)
    ln_tg = LN_ROWS // ln_tt
    x1, h2b = ln1_modulate(x3, y1, g1, sc2, sh2, lw["ln1_g"], lw["ln1_b"], ln_tg, ln_tt, alpha)

    qp = matmul([(h2b, lw["peer_wq"], 0, 0)], lw["peer_wq"].shape[1], name="peer_query")
    if "table" not in lw:
        eidx, gate, lw["table"] = peer_select_pack(qp, lw["subkeys"], lw["peer_u"], lw["peer_v"])
    else:
        eidx, gate = peer_select(qp, lw["subkeys"])
    out = peer_mix_ln2(eidx, gate, lw["table"], x1.reshape(g, t, d), sc2, sh2, g2, lw["ln2_g"], lw["ln2_b"], alpha)
    return out, (new_k, new_v, new_pool)


def kernel(x_prompt, x_sample, cache_k, cache_v, state_pool, c_prompt, c_sample, w_ada, b_ada, w_in, sinks,
           pool_w, pool_scale, w_out, ln1_g, ln1_b, peer_wq, peer_subkeys, peer_u, peer_v, ln2_g, ln2_b):
    depth = w_ada.shape[0]
    bp, seq, d = x_prompt.shape
    bs, t_new, _ = x_sample.shape
    past_len = PAST_LEN
    alpha = (2 * depth) ** 0.25
    aw = N_HEADS * HEAD_DIM
    n_c = bp + bs
    pad = (-n_c) % 8

    xp, xs = x_prompt, x_sample
    outs = [[] for _ in range(6)]
    for l in range(depth):
        c_all = jnp.concatenate([c_prompt, c_sample, jnp.zeros((pad, d), F32)], axis=0)
        mod = adaln(c_all, w_ada[l], b_ada[l][None, :])
        mod_p = mod[:bp].reshape(bp, N_MOD, 1, d)
        mod_s = mod[bp:n_c].reshape(bs, N_MOD, 1, d)
        mods_p = tuple(mod_p[:, i] for i in range(N_MOD))
        mods_s = tuple(mod_s[:, i] for i in range(N_MOD))
        lw = dict(
            alpha=alpha, w_in=w_in[l], w_out=w_out[l], peer_wq=peer_wq[l],
            sinks=sinks[l], pool_w=pool_w[l].astype(BF16), pool_scale=pool_scale[l][None, :],
            ln1_g=ln1_g[l][None, :], ln1_b=ln1_b[l][None, :], ln2_g=ln2_g[l][None, :], ln2_b=ln2_b[l][None, :],
            subkeys=peer_subkeys[l].astype(BF16), peer_u=peer_u[l], peer_v=peer_v[l],
        )
        xp, (k1, v1, p1) = _layer(xp, mods_p, lw, 0, 1, 512)
        xs, (k2, v2, p2) = _layer(xs, mods_s, lw, past_len, 64, t_new, prev=(cache_k[l], cache_v[l], state_pool[l]))
        for lst, val in zip(outs, (k1, v1, p1, k2, v2, p2)):
            lst.append(val)
    return (xp, xs) + tuple(jnp.stack(o) for o in outs)
```

```python
import functools

import jax
import jax.numpy as jnp
from jax import lax
from jax.experimental import pallas as pl
from jax.experimental.pallas import tpu as pltpu

F32 = jnp.float32
BF16 = jnp.bfloat16

LANES = 128
HEAD_DIM = 64
N_HEADS = 32
N_KV_HEADS = 4
GROUP = N_HEADS // N_KV_HEADS
KV_WIDTH = N_KV_HEADS * HEAD_DIM
WINDOW = 128
ROT_DIM = HEAD_DIM // 4
ROT_HALF = ROT_DIM // 2
ROPE_THETA = 500000.0
POOL_WINDOWS = (2, 4, 8, 16)
POOL_GROUP_WIDTH = 512
POOL_HALO = 16
POOL_STATE = 15
PEER_HEADS = 8
PEER_HALF = 128
N_KEYS = 128
PEER_TOPK = 16
PEER_K = PEER_HEADS * PEER_TOPK
N_MOD = 6
PAST_LEN = 8192
LN_ROWS = 128
POOL_TG = 16
LN_EPS = 1e-5
NEG_INF = -1e30


def _params(vmem_mb, sem=None):
    return pltpu.CompilerParams(dimension_semantics=sem, vmem_limit_bytes=vmem_mb << 20)


def _ada_kernel(c_ref, w_ref, b_ref, o_ref):
    c = c_ref[...]
    s = (c * jax.nn.sigmoid(c)).astype(BF16)
    o_ref[...] = jnp.dot(s, w_ref[...].astype(BF16), preferred_element_type=F32) + b_ref[...]


def adaln(c_all, w_ada, b_ada, tn=512):
    r, d = c_all.shape
    n = w_ada.shape[1]
    per = d // tn
    return pl.pallas_call(
        _ada_kernel,
        grid=(n // tn,),
        in_specs=[pl.BlockSpec((r, d), lambda j: (0, 0)),
                  pl.BlockSpec((d, tn), lambda j: (0, j)),
                  pl.BlockSpec((1, tn), lambda j: (0, j))],
        out_specs=pl.BlockSpec((None, r, tn), lambda j: (j // per, 0, j % per)),
        out_shape=jax.ShapeDtypeStruct((n // d, r, d), F32),
        compiler_params=_params(48, ("arbitrary",)),
        name="adaln",
    )(c_all, w_ada, b_ada)


def _mod_kernel(x_ref, sc_ref, sh_ref, o_ref):
    h = x_ref[...] * (1.0 + sc_ref[...]) + sh_ref[...]
    o_ref[...] = h.reshape(o_ref.shape).astype(o_ref.dtype)


def modulate(x3, sc, sh, tg, tt):
    g, t, d = x3.shape
    nt = t // tt
    return pl.pallas_call(
        _mod_kernel,
        grid=(g // tg, nt),
        in_specs=[pl.BlockSpec((tg, tt, d), lambda a, b: (a, b, 0)),
                  pl.BlockSpec((tg, 1, d), lambda a, b: (a, 0, 0)),
                  pl.BlockSpec((tg, 1, d), lambda a, b: (a, 0, 0))],
        out_specs=pl.BlockSpec((tg * tt, d), lambda a, b: (a * nt + b, 0)),
        out_shape=jax.ShapeDtypeStruct((g * t, d), BF16),
        compiler_params=_params(48, ("arbitrary", "arbitrary")),
        name="modulate",
    )(x3, sc, sh)


def _mm_kernel(*refs, n_pairs):
    o_ref = refs[-1]
    acc = None
    for p in range(n_pairs):
        part = jnp.dot(refs[2 * p][...], refs[2 * p + 1][...].astype(BF16), preferred_element_type=F32)
        acc = part if acc is None else acc + part
    o_ref[...] = acc.astype(o_ref.dtype)


def matmul(pairs, n, tm=1024, tn=512, out_dtype=F32, name="matmul"):
    m = pairs[0][0].shape[0]
    tm, tn = min(tm, m), min(tn, n)
    in_specs, args = [], []
    for x, w, r, c in pairs:
        k = x.shape[1]
        assert c % tn == 0 and w.shape[0] % k == 0
        cb = c // tn
        in_specs += [pl.BlockSpec((tm, k), lambda i, j: (i, 0)),
                     pl.BlockSpec((k, tn), lambda i, j, r=r, cb=cb: (r, cb + j))]
        args += [x, w]
    return pl.pallas_call(
        functools.partial(_mm_kernel, n_pairs=len(pairs)),
        grid=(m // tm, n // tn),
        in_specs=in_specs,
        out_specs=pl.BlockSpec((tm, tn), lambda i, j: (i, j)),
        out_shape=jax.ShapeDtypeStruct((m, n), out_dtype),
        compiler_params=_params(48, ("arbitrary", "arbitrary")),
        name=name,
    )(*args)


def rope_tables(pos):
    inv_freq = ROPE_THETA ** (-jnp.arange(ROT_HALF, dtype=F32) / ROT_HALF)
    ang = pos.astype(F32)[:, None] * inv_freq[None, :]
    cos, sin = jnp.cos(ang), jnp.sin(ang)
    t = pos.shape[0]
    ones = jnp.ones((t, HEAD_DIM - ROT_DIM), F32)
    zeros_h = jnp.zeros((t, ROT_HALF), F32)
    zeros_r = jnp.zeros((t, HEAD_DIM - ROT_DIM), F32)
    c = jnp.concatenate([cos, cos, ones], axis=1)
    s1 = jnp.concatenate([zeros_h, sin, zeros_r], axis=1)
    s2 = jnp.concatenate([-sin, zeros_h, zeros_r], axis=1)
    rep = LANES // HEAD_DIM
    return tuple(jnp.tile(a, (1, rep)) for a in (c, s1, s2))


def _rope(x, c, s1, s2):
    rows, w = x.shape
    reps = (rows // c.shape[0], w // LANES)
    ct, s1t, s2t = (jnp.tile(a, reps) for a in (c, s1, s2))
    return x * ct + pltpu.roll(x, ROT_HALF, 1) * s1t + pltpu.roll(x, w - ROT_HALF, 1) * s2t


def _sink_softmax_pv(pieces, sink):
    m = sink
    for s, _ in pieces:
        m = jnp.maximum(m, s.max(-1, keepdims=True))
    denom = jnp.exp(sink - m)
    o = None
    for s, v in pieces:
        p = jnp.exp(s - m)
        denom = denom + p.sum(-1, keepdims=True)
        pv = jnp.dot(p.astype(BF16), v, preferred_element_type=F32)
        o = pv if o is None else o + pv
    return o / denom


_NT = (((1,), (1,)), ((), ()))


def _attn_prompt_kernel(q_ref, kp_ref, kc_ref, vp_ref, vc_ref, cc_ref, s1c_ref, s2c_ref,
                        cp_ref, s1p_ref, s2p_ref, sink_ref, o_ref, krot_ref):
    i = pl.program_id(1)
    q = (_rope(q_ref[...], cc_ref[...], s1c_ref[...], s2c_ref[...]) * (HEAD_DIM ** -0.5)).astype(BF16)
    kc = _rope(kc_ref[...], cc_ref[...], s1c_ref[...], s2c_ref[...])
    krot_ref[...] = kc
    kp = _rope(kp_ref[...], cp_ref[...], s1p_ref[...], s2p_ref[...])
    k = jnp.concatenate([kp, kc], axis=0).astype(BF16)
    v = jnp.concatenate([vp_ref[...], vc_ref[...]], axis=0).astype(BF16)
    gr = GROUP * WINDOW
    r = lax.broadcasted_iota(jnp.int32, (gr, 2 * WINDOW), 0) % WINDOW
    j = lax.broadcasted_iota(jnp.int32, (gr, 2 * WINDOW), 1)
    diff = WINDOW + r - j
    mask = (diff >= 0) & (diff <= WINDOW) & ((j >= WINDOW) | (i > 0))
    outs = []
    for h in range(N_KV_HEADS):
        hs = slice(h * HEAD_DIM, (h + 1) * HEAD_DIM)
        qg = jnp.concatenate([q[:, (h * GROUP + g) * HEAD_DIM:(h * GROUP + g + 1) * HEAD_DIM]
                              for g in range(GROUP)], axis=0)
        sink = jnp.concatenate([jnp.full((WINDOW, 1), sink_ref[h * GROUP + g], F32) for g in range(GROUP)], axis=0)
        s = jnp.where(mask, lax.dot_general(qg, k[:, hs], _NT, preferred_element_type=F32), NEG_INF)
        o = _sink_softmax_pv([(s, v[:, hs])], sink)
        outs += [o[g * WINDOW:(g + 1) * WINDOW, :] for g in range(GROUP)]
    o_ref[...] = jnp.concatenate(outs, axis=1).astype(o_ref.dtype)


def attn_prompt(q, kv, tabs, sinks, batch, seq):
    nb = seq // WINDOW
    aw = N_HEADS * HEAD_DIM
    cur = lambda b, i: (b * nb + i, 0)
    prev = lambda b, i: (b * nb + jnp.maximum(i - 1, 0), 0)
    cur_v = lambda b, i: (b * nb + i, 1)
    prev_v = lambda b, i: (b * nb + jnp.maximum(i - 1, 0), 1)
    tcur = lambda b, i: (i, 0)
    tprev = lambda b, i: (jnp.maximum(i - 1, 0), 0)
    tspec = lambda f: pl.BlockSpec((WINDOW, LANES), f)
    return pl.pallas_call(
        _attn_prompt_kernel,
        grid=(batch, nb),
        in_specs=[pl.BlockSpec((WINDOW, aw), cur),
                  pl.BlockSpec((WINDOW, KV_WIDTH), prev), pl.BlockSpec((WINDOW, KV_WIDTH), cur),
                  pl.BlockSpec((WINDOW, KV_WIDTH), prev_v), pl.BlockSpec((WINDOW, KV_WIDTH), cur_v),
                  tspec(tcur), tspec(tcur), tspec(tcur), tspec(tprev), tspec(tprev), tspec(tprev),
                  pl.BlockSpec(memory_space=pltpu.SMEM)],
        out_specs=[pl.BlockSpec((WINDOW, aw), cur), pl.BlockSpec((WINDOW, KV_WIDTH), cur)],
        out_shape=[jax.ShapeDtypeStruct((batch * seq, aw), BF16),
                   jax.ShapeDtypeStruct((batch * seq, KV_WIDTH), F32)],
        compiler_params=_params(32, ("arbitrary", "arbitrary")),
        name="attn_prompt",
    )(q, kv, kv, kv, kv, *tabs, *tabs, sinks)


def _attn_sample_kernel(q_ref, kv_ref, ck_ref, cv_ref, c_ref, s1_ref, s2_ref, sink_ref, o_ref, krot_ref,
                        qs_ref, os_ref, *, nb, t_new):
    tabs = (c_ref[...], s1_ref[...], s2_ref[...])
    qs_ref[...] = (_rope(q_ref[...], *tabs) * (HEAD_DIM ** -0.5)).astype(BF16).astype(F32)
    krot_ref[...] = _rope(kv_ref[:, :KV_WIDTH], *tabs)
    wc = ck_ref.shape[1]
    gr = GROUP * t_new
    tok = lax.broadcasted_iota(jnp.int32, (gr, wc), 0) % t_new
    diff_c = tok + wc - lax.broadcasted_iota(jnp.int32, (gr, wc), 1)
    mask_c = (diff_c >= 0) & (diff_c <= WINDOW)
    mask_n = (lax.broadcasted_iota(jnp.int32, (gr, t_new), 1)
              <= lax.broadcasted_iota(jnp.int32, (gr, t_new), 0) % t_new)
    sink_cols = [jnp.concatenate([jnp.full((t_new, 1), sink_ref[h * GROUP + g], F32) for g in range(GROUP)], axis=0)
                 for h in range(N_KV_HEADS)]

    def one(n):
        rows = pl.ds(pl.multiple_of(n * t_new, t_new), t_new)
        q = qs_ref[rows, :]
        kn = krot_ref[rows, :].astype(BF16)
        vn = kv_ref[rows, KV_WIDTH:].astype(BF16)
        kc = ck_ref[n].astype(BF16)
        vc = cv_ref[n].astype(BF16)
        outs = []
        for h in range(N_KV_HEADS):
            hs = slice(h * HEAD_DIM, (h + 1) * HEAD_DIM)
            qg = jnp.concatenate([q[:, (h * GROUP + g) * HEAD_DIM:(h * GROUP + g + 1) * HEAD_DIM]
                                  for g in range(GROUP)], axis=0).astype(BF16)
            sc = jnp.where(mask_c, lax.dot_general(qg, kc[:, hs], _NT, preferred_element_type=F32), NEG_INF)
            sn = jnp.where(mask_n, lax.dot_general(qg, kn[:, hs], _NT, preferred_element_type=F32), NEG_INF)
            o = _sink_softmax_pv([(sc, vc[:, hs]), (sn, vn[:, hs])], sink_cols[h])
            outs += [o[g * t_new:(g + 1) * t_new, :] for g in range(GROUP)]
        os_ref[rows, :] = jnp.concatenate(outs, axis=1)

    per_trip = 4

    def body(n4, carry):
        for s in range(per_trip):
            one(per_trip * n4 + s)
        return carry

    lax.fori_loop(0, nb // per_trip, body, 0)
    o_ref[...] = os_ref[...].astype(o_ref.dtype)


def attn_sample(q, kv, cache_k, cache_v, tabs, sinks, t_new, nb=8):
    rows = q.shape[0]
    n = rows // t_new
    wc = cache_k.shape[1]
    aw = N_HEADS * HEAD_DIM
    tm = nb * t_new
    blk = lambda i: (i, 0)
    full = lambda i: (0, 0)
    return pl.pallas_call(
        functools.partial(_attn_sample_kernel, nb=nb, t_new=t_new),
        grid=(n // nb,),
        in_specs=[pl.BlockSpec((tm, aw), blk), pl.BlockSpec((tm, 2 * KV_WIDTH), blk),
                  pl.BlockSpec((nb, wc, KV_WIDTH), lambda i: (i, 0, 0)),
                  pl.BlockSpec((nb, wc, KV_WIDTH), lambda i: (i, 0, 0)),
                  pl.BlockSpec((t_new, LANES), full), pl.BlockSpec((t_new, LANES), full),
                  pl.BlockSpec((t_new, LANES), full),
                  pl.BlockSpec(memory_space=pltpu.SMEM)],
        out_specs=[pl.BlockSpec((tm, aw), blk), pl.BlockSpec((tm, KV_WIDTH), blk)],
        out_shape=[jax.ShapeDtypeStruct((rows, aw), BF16), jax.ShapeDtypeStruct((rows, KV_WIDTH), F32)],
        scratch_shapes=[pltpu.VMEM((tm, aw), F32), pltpu.VMEM((tm, aw), F32)],
        compiler_params=_params(32, ("arbitrary",)),
        name="attn_sample",
    )(q, kv, cache_k, cache_v, *tabs, sinks)


def _pool_kernel(u_ref, halo_ref, st_ref, pw_ref, scale_ref, o_ref, ext_ref, *, pos0):
    ti = pl.program_id(1)
    tg, tt, _ = u_ref.shape
    ext_ref[:, 0:POOL_HALO, :] = jnp.where(ti == 0, st_ref[...], halo_ref[...])
    ext_ref[:, POOL_HALO:, :] = u_ref[...]
    pos = pos0 + ti * tt + lax.broadcasted_iota(jnp.int32, (1, tt, 1), 1)
    for g, w in enumerate(POOL_WINDOWS):
        cols = slice(g * POOL_GROUP_WIDTH, (g + 1) * POOL_GROUP_WIDTH)
        acc = ext_ref[:, POOL_HALO:POOL_HALO + tt, cols]
        for k in range(1, w):
            acc = acc + ext_ref[:, POOL_HALO - k:POOL_HALO - k + tt, cols]
        inv_cnt = 1.0 / jnp.minimum(w, pos + 1).astype(F32)
        d = acc * inv_cnt - u_ref[:, :, cols]
        d = d.reshape(tg * tt, POOL_GROUP_WIDTH).astype(BF16)
        y = jnp.dot(d, pw_ref[g], preferred_element_type=F32) * scale_ref[:, cols]
        o_ref[:, cols] = y.astype(o_ref.dtype)


def pool_mix(u3, state16, pool_w, pool_scale, tg, tt, pos0):
    g, t, pw = u3.shape
    nt = t // tt
    hb = tt // POOL_HALO
    halo_src = u3 if nt > 1 else state16
    return pl.pallas_call(
        functools.partial(_pool_kernel, pos0=pos0),
        grid=(g // tg, nt),
        in_specs=[pl.BlockSpec((tg, tt, pw), lambda a, b: (a, b, 0)),
                  pl.BlockSpec((tg, POOL_HALO, pw), lambda a, b: (a, jnp.maximum(b * hb - 1, 0), 0)),
                  pl.BlockSpec((tg, POOL_HALO, pw), lambda a, b: (a, 0, 0)),
                  pl.BlockSpec(pool_w.shape, lambda a, b: (0, 0, 0)),
                  pl.BlockSpec((1, pw), lambda a, b: (0, 0))],
        out_specs=pl.BlockSpec((tg * tt, pw), lambda a, b: (a * nt + b, 0)),
        out_shape=jax.ShapeDtypeStruct((g * t, pw), BF16),
        scratch_shapes=[pltpu.VMEM((tg, POOL_HALO + tt, pw), F32)],
        compiler_params=_params(48, ("arbitrary", "arbitrary")),
        name="pool_mix",
    )(u3, halo_src, state16, pool_w, pool_scale)


def _layer_norm(z, g, b):
    mu = z.mean(-1, keepdims=True)
    zc = z - mu
    var = (zc * zc).mean(-1, keepdims=True)
    return zc * lax.rsqrt(var + LN_EPS) * g + b


def _ln1_kernel(x_ref, y_ref, g1_ref, sc_ref, sh_ref, lg_ref, lb_ref, x1_ref, hb_ref, *, alpha):
    x = x_ref[...]
    z = alpha * x + g1_ref[...] * y_ref[...].reshape(x.shape)
    x1 = _layer_norm(z, lg_ref[...], lb_ref[...])
    h = x1 * (1.0 + sc_ref[...]) + sh_ref[...]
    x1_ref[...] = x1.reshape(x1_ref.shape)
    hb_ref[...] = h.reshape(hb_ref.shape).astype(BF16)


def ln1_modulate(x3, y, g1, sc, sh, ln_g, ln_b, tg, tt, alpha):
    g, t, d = x3.shape
    nt = t // tt
    rows = lambda a, b: (a * nt + b, 0)
    grp = lambda a, b: (a, 0, 0)
    vec = lambda a, b: (0, 0)
    return pl.pallas_call(
        functools.partial(_ln1_kernel, alpha=alpha),
        grid=(g // tg, nt),
        in_specs=[pl.BlockSpec((tg, tt, d), lambda a, b: (a, b, 0)), pl.BlockSpec((tg * tt, d), rows),
                  pl.BlockSpec((tg, 1, d), grp), pl.BlockSpec((tg, 1, d), grp), pl.BlockSpec((tg, 1, d), grp),
                  pl.BlockSpec((1, d), vec), pl.BlockSpec((1, d), vec)],
        out_specs=[pl.BlockSpec((tg * tt, d), rows)] * 2,
        out_shape=[jax.ShapeDtypeStruct((g * t, d), F32), jax.ShapeDtypeStruct((g * t, d), BF16)],
        compiler_params=_params(48, ("arbitrary", "arbitrary")),
        name="ln1_modulate",
    )(x3, y, g1, sc, sh, ln_g, ln_b)


def _topk_rows(vals, k, payload=None):
    n = vals.shape[0]
    iota = lax.broadcasted_iota(jnp.int32, vals.shape, 0).astype(F32)
    out_v, out_i = [], []
    for _ in range(k):
        m = jnp.max(vals, axis=0, keepdims=True)
        am = jnp.min(jnp.where(vals == m, iota, float(n)), axis=0, keepdims=True)
        hit = iota == am
        out_v.append(m)
        out_i.append(am if payload is None else jnp.sum(jnp.where(hit, payload, 0.0), axis=0, keepdims=True))
        vals = jnp.where(hit, -jnp.inf, vals)
    return jnp.concatenate(out_v, axis=0), jnp.concatenate(out_i, axis=0)


def _peer_select_kernel(q_ref, sk_ref, idx_ref, gate_ref):
    tm = q_ref.shape[0]
    idx_rows, gate_rows = [], []
    for h in range(PEER_HEADS):
        tops = []
        for p in range(2):
            c0 = (h * 2 + p) * PEER_HALF
            qhp = q_ref[:, c0:c0 + PEER_HALF].astype(BF16)
            s_t = lax.dot_general(sk_ref[h, p], qhp, _NT, preferred_element_type=F32)
            tops.append(_topk_rows(s_t, PEER_TOPK))
        (v1, i1), (v2, i2) = tops
        nb = [PEER_TOPK // (a + 1) for a in range(PEER_TOPK)]
        pad = (-sum(nb)) % 8
        cand = jnp.concatenate([v1[a:a + 1, :] + v2[:nb[a], :] for a in range(PEER_TOPK)]
                               + [jnp.full((pad, tm), -jnp.inf, F32)], axis=0)
        cidx = jnp.concatenate([i1[a:a + 1, :] * float(N_KEYS) + i2[:nb[a], :] for a in range(PEER_TOPK)]
                               + [jnp.zeros((pad, tm), F32)], axis=0)
        best, eidx = _topk_rows(cand, PEER_TOPK, payload=cidx)
        e = jnp.exp(best - best[0:1, :])
        gate_rows.append(e / e.sum(axis=0, keepdims=True))
        idx_rows.append(eidx)
    idx_ref[...] = jnp.concatenate(idx_rows, axis=0).T.astype(jnp.int32)
    gate_ref[...] = jnp.concatenate(gate_rows, axis=0).T


def peer_select(qp, subkeys_bf16, tm=128):
    rows, w = qp.shape
    tm = min(tm, rows)
    blk = lambda i: (i, 0)
    return pl.pallas_call(
        _peer_select_kernel,
        grid=(rows // tm,),
        in_specs=[pl.BlockSpec((tm, w), blk), pl.BlockSpec(subkeys_bf16.shape, lambda i: (0, 0, 0, 0))],
        out_specs=[pl.BlockSpec((tm, PEER_K), blk), pl.BlockSpec((tm, PEER_K), blk)],
        out_shape=[jax.ShapeDtypeStruct((rows, PEER_K), jnp.int32), jax.ShapeDtypeStruct((rows, PEER_K), F32)],
        compiler_params=_params(32, ("arbitrary",)),
        name="peer_select",
    )(qp, subkeys_bf16)


def _pack_kernel(u_ref, v_ref, o_ref):
    half = u_ref.shape[1] // 2
    o_ref[:, 0, :half] = u_ref[:, :half].astype(BF16)
    o_ref[:, 0, half:] = v_ref[:, :half].astype(BF16)
    o_ref[:, 1, :half] = u_ref[:, half:].astype(BF16)
    o_ref[:, 1, half:] = v_ref[:, half:].astype(BF16)


def _select_pack_kernel(q_ref, sk_ref, u_ref, v_ref, idx_ref, gate_ref, tab_ref):
    _peer_select_kernel(q_ref, sk_ref, idx_ref, gate_ref)
    _pack_kernel(u_ref, v_ref, tab_ref)


def peer_select_pack(qp, subkeys_bf16, peer_u, peer_v, tm=128):
    rows, w = qp.shape
    tm = min(tm, rows)
    steps = rows // tm
    e, d = peer_u.shape
    assert e % steps == 0
    tr = e // steps
    blk = lambda i: (i, 0)
    return pl.pallas_call(
        _select_pack_kernel,
        grid=(steps,),
        in_specs=[pl.BlockSpec((tm, w), blk), pl.BlockSpec(subkeys_bf16.shape, lambda i: (0, 0, 0, 0)),
                  pl.BlockSpec((tr, d), blk), pl.BlockSpec((tr, d), blk)],
        out_specs=[pl.BlockSpec((tm, PEER_K), blk), pl.BlockSpec((tm, PEER_K), blk),
                   pl.BlockSpec((tr, 2, d), lambda i: (i, 0, 0))],
        out_shape=[jax.ShapeDtypeStruct((rows, PEER_K), jnp.int32), jax.ShapeDtypeStruct((rows, PEER_K), F32),
                   jax.ShapeDtypeStruct((e, 2, d), BF16)],
        compiler_params=_params(48, ("arbitrary",)),
        name="peer_select_pack",
    )(qp, subkeys_bf16, peer_u, peer_v)


def _peer_mix_kernel(idx_ref, nidx_ref, gate_ref, tab_ref, x1_ref, sc_ref, sh_ref, g2_ref, lg_ref, lb_ref, o_ref,
                     buf_a, buf_b, sem_a, sem_b, x_ref, y_ref, *, tt, nsteps, alpha):
    i = pl.program_id(0)
    half = x_ref.shape[1] // 2
    nchunk = half // LANES
    x_ref[...] = (x1_ref[...] * (1.0 + sc_ref[...]) + sh_ref[...]).reshape(x_ref.shape)

    def issue(ids, row, buf, sem, t):
        for k in range(PEER_K):
            pltpu.make_async_copy(tab_ref.at[ids[row, k]], buf.at[t, pl.ds(2 * k, 2), :], sem.at[t]).start()

    def wait(buf, sem, t):
        pltpu.make_async_copy(buf.at[t], buf.at[t], sem.at[t]).wait()

    @pl.when(i == 0)
    def _():
        def first(t, c):
            issue(idx_ref, t, buf_a, sem_a, t)
            return c
        lax.fori_loop(0, tt, first, 0)

    kr = 2 * PEER_K
    sub = lax.broadcasted_iota(jnp.int32, (kr, LANES), 0)
    even = sub % 2 == 0
    pair_eye = sub // 2 == lax.broadcasted_iota(jnp.int32, (kr, LANES), 1)
    row_id = lax.broadcasted_iota(jnp.int32, (tt, LANES), 0)
    y_ref[...] = jnp.zeros(y_ref.shape, F32)

    def mix(buf, t, row0):
        x = x_ref[pl.ds(row0 + t, 1), :]
        acc = jnp.zeros((kr // 8, 8, LANES), F32)
        for ch in range(nchunk):
            cs = slice(ch * LANES, (ch + 1) * LANES)
            xs = jnp.where(even[:8], x[:, cs], x[:, half + ch * LANES:half + (ch + 1) * LANES])
            acc = acc + buf[t, :, cs].astype(F32).reshape(kr // 8, 8, LANES) * xs[None]
        acc = acc.reshape(kr, LANES)
        acc = acc + jnp.where(even, pltpu.roll(acc, kr - 1, 0), pltpu.roll(acc, 1, 0))
        a = acc.sum(axis=1, keepdims=True)
        gate = jnp.sum(jnp.where(pair_eye, gate_ref[pl.ds(row0 + t, 1), :], 0.0), axis=1, keepdims=True)
        act = 0.5 * a * (1.0 + lax.erf(a * (2.0 ** -0.5))) * gate
        act_b = jnp.broadcast_to(act, (kr, LANES))
        mine = row_id == t
        rows = slice(row0, row0 + tt)
        for ch in range(nchunk):
            cs = slice(half + ch * LANES, half + (ch + 1) * LANES)
            s = (act_b * buf[t, :, cs].astype(F32)).reshape(kr // 8, 8, LANES).sum(axis=0)
            s = s + pltpu.roll(s, 2, 0)
            s = s + pltpu.roll(s, 4, 0)
            for r, c0 in ((0, ch * LANES), (1, half + ch * LANES)):
                ys = slice(c0, c0 + LANES)
                y_ref[rows, ys] = jnp.where(mine, s[r:r + 1, :], y_ref[rows, ys])

    def token_a(t, c):
        wait(buf_a, sem_a, t)
        issue(idx_ref, tt + t, buf_b, sem_b, t)
        mix(buf_a, t, 0)
        return c

    def token_b(t, c):
        wait(buf_b, sem_b, t)
        issue(nidx_ref, t, buf_a, sem_a, t)
        mix(buf_b, t, tt)
        return c

    lax.fori_loop(0, tt, token_a, 0)
    lax.fori_loop(0, tt, token_b, 0)

    @pl.when(i == nsteps - 1)
    def _():
        for t in range(tt):
            wait(buf_a, sem_a, t)

    z = alpha * x1_ref[...] + g2_ref[...] * y_ref[...].reshape(x1_ref.shape)
    o_ref[...] = _layer_norm(z, lg_ref[...], lb_ref[...])


def peer_mix_ln2(eidx, gate, table, x1_3, sc, sh, g2, ln_g, ln_b, alpha, tt=8):
    g, t, d = x1_3.shape
    rows = g * t
    tb = 2 * tt
    nsteps = rows // tb
    bt = min(tb, t)
    bg = tb // bt
    nt = t // bt
    blk = lambda i: (i, 0)
    nxt = lambda i: ((i + 1) % nsteps, 0)
    tok3 = pl.BlockSpec((bg, bt, d), lambda i: (i // nt, i % nt, 0))
    grp = pl.BlockSpec((bg, 1, d), lambda i: (i // nt, 0, 0))
    vec = pl.BlockSpec((1, d), lambda i: (0, 0))
    return pl.pallas_call(
        functools.partial(_peer_mix_kernel, tt=tt, nsteps=nsteps, alpha=alpha),
        grid=(nsteps,),
        in_specs=[pl.BlockSpec((tb, PEER_K), blk, memory_space=pltpu.SMEM),
                  pl.BlockSpec((tb, PEER_K), nxt, memory_space=pltpu.SMEM),
                  pl.BlockSpec((tb, PEER_K), blk),
                  pl.BlockSpec(memory_space=pl.ANY),
                  tok3, grp, grp, grp, vec, vec],
        out_specs=tok3,
        out_shape=jax.ShapeDtypeStruct((g, t, d), F32),
        scratch_shapes=[pltpu.VMEM((tt, 2 * PEER_K, d), BF16), pltpu.VMEM((tt, 2 * PEER_K, d), BF16),
                        pltpu.SemaphoreType.DMA((tt,)), pltpu.SemaphoreType.DMA((tt,)),
                        pltpu.VMEM((tb, d), F32), pltpu.VMEM((tb, d), F32)],
        compiler_params=_params(56, ("arbitrary",)),
        name="peer_mix",
    )(eidx, eidx, gate, table, x1_3, sc, sh, g2, ln_g, ln_b)


def _layer(x3, mods, lw, pos0, tg, tt, prev=None):
    g, t, d = x3.shape
    sh1, sc1, g1, sh2, sc2, g2 = mods
    aw = N_HEADS * HEAD_DIM
    alpha = lw["alpha"]

    h = modulate(x3, sc1, sh1, tg, tt)
    q = matmul([(h, lw["w_in"], 0, 0)], aw, name="in_q")
    kv = matmul([(h, lw["w_in"], 0, aw)], 2 * KV_WIDTH, name="in_kv")
    u = matmul([(h, lw["w_in"], 0, aw + 2 * KV_WIDTH)], lw["w_in"].shape[1] - aw - 2 * KV_WIDTH, name="in_u")

    tabs = rope_tables(pos0 + jnp.arange(t, dtype=jnp.int32))
    if prev is None:
        attn, krot = attn_prompt(q, kv, tabs, lw["sinks"], g, t)
        state16 = jnp.zeros((g, POOL_HALO, u.shape[1]), F32)
        new_k = krot.reshape(g, t, KV_WIDTH)[:, -WINDOW:].reshape(g, WINDOW, N_KV_HEADS, HEAD_DIM)
        new_v = kv.reshape(g, t, 2 * KV_WIDTH)[:, -WINDOW:, KV_WIDTH:].reshape(g, WINDOW, N_KV_HEADS, HEAD_DIM)
    else:
        cache_k, cache_v, state = prev
        wc = cache_k.shape[1]
        attn, krot = attn_sample(q, kv, cache_k.reshape(g, wc, KV_WIDTH), cache_v.reshape(g, wc, KV_WIDTH),
                                 tabs, lw["sinks"], t)
        state16 = jnp.pad(state, ((0, 0), (POOL_HALO - POOL_STATE, 0), (0, 0)))
        new_k = jnp.concatenate([cache_k, krot.reshape(g, t, N_KV_HEADS, HEAD_DIM)], axis=1)[:, -wc:]
        new_v = jnp.concatenate([cache_v, kv[:, KV_WIDTH:].reshape(g, t, N_KV_HEADS, HEAD_DIM)], axis=1)[:, -wc:]
    u3 = u.reshape(g, t, u.shape[1])
    new_pool = jnp.concatenate([state16, u3], axis=1)[:, -POOL_STATE:]
    pooled = pool_mix(u3, state16, lw["pool_w"], lw["pool_scale"], min(tg, POOL_TG), tt, pos0)

    y1 = matmul([(attn, lw["w_out"], 0, 0), (pooled, lw["w_out"], 1, 0)], d, name="out_proj")
    ln_tt = min(t, LN_ROWS)
    ln_tg = LN_ROWS // ln_tt
    x1, h2b = ln1_modulate(x3, y1, g1, sc2, sh2, lw["ln1_g"], lw["ln1_b"], ln_tg, ln_tt, alpha)

    qp = matmul([(h2b, lw["peer_wq"], 0, 0)], lw["peer_wq"].shape[1], name="peer_query")
    if "table" not in lw:
        eidx, gate, lw["table"] = peer_select_pack(qp, lw["subkeys"], lw["peer_u"], lw["peer_v"])
    else:
        eidx, gate = peer_select(qp, lw["subkeys"])
    out = peer_mix_ln2(eidx, gate, lw["table"], x1.reshape(g, t, d), sc2, sh2, g2, lw["ln2_g"], lw["ln2_b"], alpha)
    return out, (new_k, new_v, new_pool)


def kernel(x_prompt, x_sample, cache_k, cache_v, state_pool, c_prompt, c_sample, w_ada, b_ada, w_in, sinks,
           pool_w, pool_scale, w_out, ln1_g, ln1_b, peer_wq, peer_subkeys, peer_u, peer_v, ln2_g, ln2_b):
    depth = w_ada.shape[0]
    bp, seq, d = x_prompt.shape
    bs, t_new, _ = x_sample.shape
    past_len = PAST_LEN
    alpha = (2 * depth) ** 0.25
    aw = N_HEADS * HEAD_DIM
    n_c = bp + bs
    pad = (-n_c) % 8

    xp, xs = x_prompt, x_sample
    outs = [[] for _ in range(6)]
    for l in range(depth):
        c_all = jnp.concatenate([c_prompt, c_sample, jnp.zeros((pad, d), F32)], axis=0)
        mod = adaln(c_all, w_ada[l], b_ada[l][None, :])
        mods_p = tuple(mod[i, :bp].reshape(bp, 1, d) for i in range(N_MOD))
        mods_s = tuple(mod[i, bp:n_c].reshape(bs, 1, d) for i in range(N_MOD))
        lw = dict(
            alpha=alpha, w_in=w_in[l], w_out=w_out[l], peer_wq=peer_wq[l],
            sinks=sinks[l], pool_w=pool_w[l].astype(BF16), pool_scale=pool_scale[l][None, :],
            ln1_g=ln1_g[l][None, :], ln1_b=ln1_b[l][None, :], ln2_g=ln2_g[l][None, :], ln2_b=ln2_b[l][None, :],
            subkeys=peer_subkeys[l].astype(BF16), peer_u=peer_u[l], peer_v=peer_v[l],
        )
        xp, (k1, v1, p1) = _layer(xp, mods_p, lw, 0, 1, 512)
        xs, (k2, v2, p2) = _layer(xs, mods_s, lw, past_len, 64, t_new, prev=(cache_k[l], cache_v[l], state_pool[l]))
        for lst, val in zip(outs, (k1, v1, p1, k2, v2, p2)):
            lst.append(val)
    return (xp, xs) + tuple(jnp.stack(o) for o in outs)
```

```python
import functools

import jax
import jax.numpy as jnp
from jax import lax
from jax.experimental import pallas as pl
from jax.experimental.pallas import tpu as pltpu

F32 = jnp.float32
BF16 = jnp.bfloat16

LANES = 128
HEAD_DIM = 64
N_HEADS = 32
N_KV_HEADS = 4
GROUP = N_HEADS // N_KV_HEADS
KV_WIDTH = N_KV_HEADS * HEAD_DIM
WINDOW = 128
ROT_DIM = HEAD_DIM // 4
ROT_HALF = ROT_DIM // 2
ROPE_THETA = 500000.0
POOL_WINDOWS = (2, 4, 8, 16)
POOL_GROUP_WIDTH = 512
POOL_HALO = 16
POOL_STATE = 15
PEER_HEADS = 8
PEER_HALF = 128
N_KEYS = 128
PEER_TOPK = 16
PEER_K = PEER_HEADS * PEER_TOPK
N_MOD = 6
PAST_LEN = 8192
LN_ROWS = 128
POOL_TG = 16
LN_EPS = 1e-5
NEG_INF = -1e30


def _params(vmem_mb, sem=None):
    return pltpu.CompilerParams(dimension_semantics=sem, vmem_limit_bytes=vmem_mb << 20)


def _ada_kernel(c_ref, w_ref, b_ref, o_ref):
    c = c_ref[...]
    s = (c * jax.nn.sigmoid(c)).astype(BF16)
    o_ref[...] = jnp.dot(s, w_ref[...].astype(BF16), preferred_element_type=F32) + b_ref[...]


def adaln(c_all, w_ada, b_ada, tn=512):
    r, d = c_all.shape
    n = w_ada.shape[1]
    per = d // tn
    return pl.pallas_call(
        _ada_kernel,
        grid=(n // tn,),
        in_specs=[pl.BlockSpec((r, d), lambda j: (0, 0)),
                  pl.BlockSpec((d, tn), lambda j: (0, j)),
                  pl.BlockSpec((1, tn), lambda j: (0, j))],
        out_specs=pl.BlockSpec((None, r, tn), lambda j: (j // per, 0, j % per)),
        out_shape=jax.ShapeDtypeStruct((n // d, r, d), F32),
        compiler_params=_params(48, ("arbitrary",)),
        name="adaln",
    )(c_all, w_ada, b_ada)


def _mod_kernel(x_ref, sc_ref, sh_ref, o_ref):
    h = x_ref[...] * (1.0 + sc_ref[...]) + sh_ref[...]
    o_ref[...] = h.reshape(o_ref.shape).astype(o_ref.dtype)


def modulate(x3, sc, sh, tg, tt):
    g, t, d = x3.shape
    nt = t // tt
    return pl.pallas_call(
        _mod_kernel,
        grid=(g // tg, nt),
        in_specs=[pl.BlockSpec((tg, tt, d), lambda a, b: (a, b, 0)),
                  pl.BlockSpec((tg, 1, d), lambda a, b: (a, 0, 0)),
                  pl.BlockSpec((tg, 1, d), lambda a, b: (a, 0, 0))],
        out_specs=pl.BlockSpec((tg * tt, d), lambda a, b: (a * nt + b, 0)),
        out_shape=jax.ShapeDtypeStruct((g * t, d), BF16),
        compiler_params=_params(48, ("arbitrary", "arbitrary")),
        name="modulate",
    )(x3, sc, sh)


def _mm_kernel(*refs, n_pairs):
    o_ref = refs[-1]
    acc = None
    for p in range(n_pairs):
        part = jnp.dot(refs[2 * p][...], refs[2 * p + 1][...].astype(BF16), preferred_element_type=F32)
        acc = part if acc is None else acc + part
    o_ref[...] = acc.astype(o_ref.dtype)


def matmul(pairs, n, tm=1024, tn=512, out_dtype=F32, name="matmul"):
    m = pairs[0][0].shape[0]
    tm, tn = min(tm, m), min(tn, n)
    in_specs, args = [], []
    for x, w, r, c in pairs:
        k = x.shape[1]
        assert c % tn == 0 and w.shape[0] % k == 0
        cb = c // tn
        in_specs += [pl.BlockSpec((tm, k), lambda i, j: (i, 0)),
                     pl.BlockSpec((k, tn), lambda i, j, r=r, cb=cb: (r, cb + j))]
        args += [x, w]
    return pl.pallas_call(
        functools.partial(_mm_kernel, n_pairs=len(pairs)),
        grid=(m // tm, n // tn),
        in_specs=in_specs,
        out_specs=pl.BlockSpec((tm, tn), lambda i, j: (i, j)),
        out_shape=jax.ShapeDtypeStruct((m, n), out_dtype),
        compiler_params=_params(48, ("arbitrary", "arbitrary")),
        name=name,
    )(*args)


def rope_tables(pos):
    inv_freq = ROPE_THETA ** (-jnp.arange(ROT_HALF, dtype=F32) / ROT_HALF)
    ang = pos.astype(F32)[:, None] * inv_freq[None, :]
    cos, sin = jnp.cos(ang), jnp.sin(ang)
    t = pos.shape[0]
    ones = jnp.ones((t, HEAD_DIM - ROT_DIM), F32)
    zeros_h = jnp.zeros((t, ROT_HALF), F32)
    zeros_r = jnp.zeros((t, HEAD_DIM - ROT_DIM), F32)
    c = jnp.concatenate([cos, cos, ones], axis=1)
    s1 = jnp.concatenate([zeros_h, sin, zeros_r], axis=1)
    s2 = jnp.concatenate([-sin, zeros_h, zeros_r], axis=1)
    rep = LANES // HEAD_DIM
    return tuple(jnp.tile(a, (1, rep)) for a in (c, s1, s2))


def _rope(x, c, s1, s2):
    rows, w = x.shape
    reps = (rows // c.shape[0], w // LANES)
    ct, s1t, s2t = (jnp.tile(a, reps) for a in (c, s1, s2))
    return x * ct + pltpu.roll(x, ROT_HALF, 1) * s1t + pltpu.roll(x, w - ROT_HALF, 1) * s2t


def _sink_softmax_pv(pieces, sink):
    m = sink
    for s, _ in pieces:
        m = jnp.maximum(m, s.max(-1, keepdims=True))
    denom = jnp.exp(sink - m)
    o = None
    for s, v in pieces:
        p = jnp.exp(s - m)
        denom = denom + p.sum(-1, keepdims=True)
        pv = jnp.dot(p.astype(BF16), v, preferred_element_type=F32)
        o = pv if o is None else o + pv
    return o / denom


_NT = (((1,), (1,)), ((), ()))


def _attn_prompt_kernel(q_ref, kp_ref, kc_ref, vp_ref, vc_ref, cc_ref, s1c_ref, s2c_ref,
                        cp_ref, s1p_ref, s2p_ref, sink_ref, o_ref, krot_ref):
    i = pl.program_id(1)
    q = (_rope(q_ref[...], cc_ref[...], s1c_ref[...], s2c_ref[...]) * (HEAD_DIM ** -0.5)).astype(BF16)
    kc = _rope(kc_ref[...], cc_ref[...], s1c_ref[...], s2c_ref[...])
    krot_ref[...] = kc
    kp = _rope(kp_ref[...], cp_ref[...], s1p_ref[...], s2p_ref[...])
    k = jnp.concatenate([kp, kc], axis=0).astype(BF16)
    v = jnp.concatenate([vp_ref[...], vc_ref[...]], axis=0).astype(BF16)
    gr = GROUP * WINDOW
    r = lax.broadcasted_iota(jnp.int32, (gr, 2 * WINDOW), 0) % WINDOW
    j = lax.broadcasted_iota(jnp.int32, (gr, 2 * WINDOW), 1)
    diff = WINDOW + r - j
    mask = (diff >= 0) & (diff <= WINDOW) & ((j >= WINDOW) | (i > 0))
    outs = []
    for h in range(N_KV_HEADS):
        hs = slice(h * HEAD_DIM, (h + 1) * HEAD_DIM)
        qg = jnp.concatenate([q[:, (h * GROUP + g) * HEAD_DIM:(h * GROUP + g + 1) * HEAD_DIM]
                              for g in range(GROUP)], axis=0)
        sink = jnp.concatenate([jnp.full((WINDOW, 1), sink_ref[h * GROUP + g], F32) for g in range(GROUP)], axis=0)
        s = jnp.where(mask, lax.dot_general(qg, k[:, hs], _NT, preferred_element_type=F32), NEG_INF)
        o = _sink_softmax_pv([(s, v[:, hs])], sink)
        outs += [o[g * WINDOW:(g + 1) * WINDOW, :] for g in range(GROUP)]
    o_ref[...] = jnp.concatenate(outs, axis=1).astype(o_ref.dtype)


def attn_prompt(q, kv, tabs, sinks, batch, seq):
    nb = seq // WINDOW
    aw = N_HEADS * HEAD_DIM
    cur = lambda b, i: (b * nb + i, 0)
    prev = lambda b, i: (b * nb + jnp.maximum(i - 1, 0), 0)
    cur_v = lambda b, i: (b * nb + i, 1)
    prev_v = lambda b, i: (b * nb + jnp.maximum(i - 1, 0), 1)
    tcur = lambda b, i: (i, 0)
    tprev = lambda b, i: (jnp.maximum(i - 1, 0), 0)
    tspec = lambda f: pl.BlockSpec((WINDOW, LANES), f)
    return pl.pallas_call(
        _attn_prompt_kernel,
        grid=(batch, nb),
        in_specs=[pl.BlockSpec((WINDOW, aw), cur),
                  pl.BlockSpec((WINDOW, KV_WIDTH), prev), pl.BlockSpec((WINDOW, KV_WIDTH), cur),
                  pl.BlockSpec((WINDOW, KV_WIDTH), prev_v), pl.BlockSpec((WINDOW, KV_WIDTH), cur_v),
                  tspec(tcur), tspec(tcur), tspec(tcur), tspec(tprev), tspec(tprev), tspec(tprev),
                  pl.BlockSpec(memory_space=pltpu.SMEM)],
        out_specs=[pl.BlockSpec((WINDOW, aw), cur), pl.BlockSpec((WINDOW, KV_WIDTH), cur)],
        out_shape=[jax.ShapeDtypeStruct((batch * seq, aw), BF16),
                   jax.ShapeDtypeStruct((batch * seq, KV_WIDTH), F32)],
        compiler_params=_params(32, ("arbitrary", "arbitrary")),
        name="attn_prompt",
    )(q, kv, kv, kv, kv, *tabs, *tabs, sinks)


def _attn_sample_kernel(q_ref, kv_ref, ck_ref, cv_ref, c_ref, s1_ref, s2_ref, sink_ref, o_ref, krot_ref,
                        qs_ref, os_ref, *, nb, t_new):
    tabs = (c_ref[...], s1_ref[...], s2_ref[...])
    qs_ref[...] = (_rope(q_ref[...], *tabs) * (HEAD_DIM ** -0.5)).astype(BF16).astype(F32)
    krot_ref[...] = _rope(kv_ref[:, :KV_WIDTH], *tabs)
    wc = ck_ref.shape[1]
    gr = GROUP * t_new
    tok = lax.broadcasted_iota(jnp.int32, (gr, wc), 0) % t_new
    diff_c = tok + wc - lax.broadcasted_iota(jnp.int32, (gr, wc), 1)
    mask_c = (diff_c >= 0) & (diff_c <= WINDOW)
    mask_n = (lax.broadcasted_iota(jnp.int32, (gr, t_new), 1)
              <= lax.broadcasted_iota(jnp.int32, (gr, t_new), 0) % t_new)
    sink_cols = [jnp.concatenate([jnp.full((t_new, 1), sink_ref[h * GROUP + g], F32) for g in range(GROUP)], axis=0)
                 for h in range(N_KV_HEADS)]

    def one(n):
        rows = pl.ds(pl.multiple_of(n * t_new, t_new), t_new)
        q = qs_ref[rows, :]
        kn = krot_ref[rows, :].astype(BF16)
        vn = kv_ref[rows, KV_WIDTH:].astype(BF16)
        kc = ck_ref[n].astype(BF16)
        vc = cv_ref[n].astype(BF16)
        outs = []
        for h in range(N_KV_HEADS):
            hs = slice(h * HEAD_DIM, (h + 1) * HEAD_DIM)
            qg = jnp.concatenate([q[:, (h * GROUP + g) * HEAD_DIM:(h * GROUP + g + 1) * HEAD_DIM]
                                  for g in range(GROUP)], axis=0).astype(BF16)
            sc = jnp.where(mask_c, lax.dot_general(qg, kc[:, hs], _NT, preferred_element_type=F32), NEG_INF)
            sn = jnp.where(mask_n, lax.dot_general(qg, kn[:, hs], _NT, preferred_element_type=F32), NEG_INF)
            o = _sink_softmax_pv([(sc, vc[:, hs]), (sn, vn[:, hs])], sink_cols[h])
            outs += [o[g * t_new:(g + 1) * t_new, :] for g in range(GROUP)]
        os_ref[rows, :] = jnp.concatenate(outs, axis=1)

    per_trip = 4

    def body(n4, carry):
        for s in range(per_trip):
            one(per_trip * n4 + s)
        return carry

    lax.fori_loop(0, nb // per_trip, body, 0)
    o_ref[...] = os_ref[...].astype(o_ref.dtype)


def attn_sample(q, kv, cache_k, cache_v, tabs, sinks, t_new, nb=8):
    rows = q.shape[0]
    n = rows // t_new
    wc = cache_k.shape[1]
    aw = N_HEADS * HEAD_DIM
    tm = nb * t_new
    blk = lambda i: (i, 0)
    full = lambda i: (0, 0)
    return pl.pallas_call(
        functools.partial(_attn_sample_kernel, nb=nb, t_new=t_new),
        grid=(n // nb,),
        in_specs=[pl.BlockSpec((tm, aw), blk), pl.BlockSpec((tm, 2 * KV_WIDTH), blk),
                  pl.BlockSpec((nb, wc, KV_WIDTH), lambda i: (i, 0, 0)),
                  pl.BlockSpec((nb, wc, KV_WIDTH), lambda i: (i, 0, 0)),
                  pl.BlockSpec((t_new, LANES), full), pl.BlockSpec((t_new, LANES), full),
                  pl.BlockSpec((t_new, LANES), full),
                  pl.BlockSpec(memory_space=pltpu.SMEM)],
        out_specs=[pl.BlockSpec((tm, aw), blk), pl.BlockSpec((tm, KV_WIDTH), blk)],
        out_shape=[jax.ShapeDtypeStruct((rows, aw), BF16), jax.ShapeDtypeStruct((rows, KV_WIDTH), F32)],
        scratch_shapes=[pltpu.VMEM((tm, aw), F32), pltpu.VMEM((tm, aw), F32)],
        compiler_params=_params(32, ("arbitrary",)),
        name="attn_sample",
    )(q, kv, cache_k, cache_v, *tabs, sinks)


def _pool_kernel(u_ref, halo_ref, st_ref, pw_ref, scale_ref, o_ref, ext_ref, *, pos0):
    ti = pl.program_id(1)
    tg, tt, _ = u_ref.shape
    ext_ref[:, 0:POOL_HALO, :] = jnp.where(ti == 0, st_ref[...], halo_ref[...])
    ext_ref[:, POOL_HALO:, :] = u_ref[...]
    pos = pos0 + ti * tt + lax.broadcasted_iota(jnp.int32, (1, tt, 1), 1)
    for g, w in enumerate(POOL_WINDOWS):
        cols = slice(g * POOL_GROUP_WIDTH, (g + 1) * POOL_GROUP_WIDTH)
        acc = ext_ref[:, POOL_HALO:POOL_HALO + tt, cols]
        for k in range(1, w):
            acc = acc + ext_ref[:, POOL_HALO - k:POOL_HALO - k + tt, cols]
        inv_cnt = 1.0 / jnp.minimum(w, pos + 1).astype(F32)
        d = acc * inv_cnt - u_ref[:, :, cols]
        d = d.reshape(tg * tt, POOL_GROUP_WIDTH).astype(BF16)
        y = jnp.dot(d, pw_ref[g], preferred_element_type=F32) * scale_ref[:, cols]
        o_ref[:, cols] = y.astype(o_ref.dtype)


def pool_mix(u3, state16, pool_w, pool_scale, tg, tt, pos0):
    g, t, pw = u3.shape
    nt = t // tt
    hb = tt // POOL_HALO
    halo_src = u3 if nt > 1 else state16
    return pl.pallas_call(
        functools.partial(_pool_kernel, pos0=pos0),
        grid=(g // tg, nt),
        in_specs=[pl.BlockSpec((tg, tt, pw), lambda a, b: (a, b, 0)),
                  pl.BlockSpec((tg, POOL_HALO, pw), lambda a, b: (a, jnp.maximum(b * hb - 1, 0), 0)),
                  pl.BlockSpec((tg, POOL_HALO, pw), lambda a, b: (a, 0, 0)),
                  pl.BlockSpec(pool_w.shape, lambda a, b: (0, 0, 0)),
                  pl.BlockSpec((1, pw), lambda a, b: (0, 0))],
        out_specs=pl.BlockSpec((tg * tt, pw), lambda a, b: (a * nt + b, 0)),
        out_shape=jax.ShapeDtypeStruct((g * t, pw), BF16),
        scratch_shapes=[pltpu.VMEM((tg, POOL_HALO + tt, pw), F32)],
        compiler_params=_params(48, ("arbitrary", "arbitrary")),
        name="pool_mix",
    )(u3, halo_src, state16, pool_w, pool_scale)


def _layer_norm(z, g, b):
    mu = z.mean(-1, keepdims=True)
    zc = z - mu
    var = (zc * zc).mean(-1, keepdims=True)
    return zc * lax.rsqrt(var + LN_EPS) * g + b


def _ln1_kernel(x_ref, y_ref, g1_ref, sc_ref, sh_ref, lg_ref, lb_ref, x1_ref, hb_ref, *, alpha):
    x = x_ref[...]
    z = alpha * x + g1_ref[...] * y_ref[...].reshape(x.shape)
    x1 = _layer_norm(z, lg_ref[...], lb_ref[...])
    h = x1 * (1.0 + sc_ref[...]) + sh_ref[...]
    x1_ref[...] = x1.reshape(x1_ref.shape)
    hb_ref[...] = h.reshape(hb_ref.shape).astype(BF16)


def ln1_modulate(x3, y, g1, sc, sh, ln_g, ln_b, tg, tt, alpha):
    g, t, d = x3.shape
    nt = t // tt
    rows = lambda a, b: (a * nt + b, 0)
    grp = lambda a, b: (a, 0, 0)
    vec = lambda a, b: (0, 0)
    return pl.pallas_call(
        functools.partial(_ln1_kernel, alpha=alpha),
        grid=(g // tg, nt),
        in_specs=[pl.BlockSpec((tg, tt, d), lambda a, b: (a, b, 0)), pl.BlockSpec((tg * tt, d), rows),
                  pl.BlockSpec((tg, 1, d), grp), pl.BlockSpec((tg, 1, d), grp), pl.BlockSpec((tg, 1, d), grp),
                  pl.BlockSpec((1, d), vec), pl.BlockSpec((1, d), vec)],
        out_specs=[pl.BlockSpec((tg * tt, d), rows)] * 2,
        out_shape=[jax.ShapeDtypeStruct((g * t, d), F32), jax.ShapeDtypeStruct((g * t, d), BF16)],
        compiler_params=_params(48, ("arbitrary", "arbitrary")),
        name="ln1_modulate",
    )(x3, y, g1, sc, sh, ln_g, ln_b)


def _topk_rows(vals, k, payload=None):
    n = vals.shape[0]
    iota = lax.broadcasted_iota(jnp.int32, vals.shape, 0).astype(F32)
    out_v, out_i = [], []
    for _ in range(k):
        m = jnp.max(vals, axis=0, keepdims=True)
        am = jnp.min(jnp.where(vals == m, iota, float(n)), axis=0, keepdims=True)
        hit = iota == am
        out_v.append(m)
        out_i.append(am if payload is None else jnp.sum(jnp.where(hit, payload, 0.0), axis=0, keepdims=True))
        vals = jnp.where(hit, -jnp.inf, vals)
    return jnp.concatenate(out_v, axis=0), jnp.concatenate(out_i, axis=0)


def _peer_select_kernel(q_ref, sk_ref, idx_ref, gate_ref):
    tm = q_ref.shape[0]
    idx_rows, gate_rows = [], []
    for h in range(PEER_HEADS):
        tops = []
        for p in range(2):
            c0 = (h * 2 + p) * PEER_HALF
            qhp = q_ref[:, c0:c0 + PEER_HALF].astype(BF16)
            s_t = lax.dot_general(sk_ref[h, p], qhp, _NT, preferred_element_type=F32)
            tops.append(_topk_rows(s_t, PEER_TOPK))
        (v1, i1), (v2, i2) = tops
        nb = [PEER_TOPK // (a + 1) for a in range(PEER_TOPK)]
        pad = (-sum(nb)) % 8
        cand = jnp.concatenate([v1[a:a + 1, :] + v2[:nb[a], :] for a in range(PEER_TOPK)]
                               + [jnp.full((pad, tm), -jnp.inf, F32)], axis=0)
        cidx = jnp.concatenate([i1[a:a + 1, :] * float(N_KEYS) + i2[:nb[a], :] for a in range(PEER_TOPK)]
                               + [jnp.zeros((pad, tm), F32)], axis=0)
        best, eidx = _topk_rows(cand, PEER_TOPK, payload=cidx)
        e = jnp.exp(best - best[0:1, :])
        gate_rows.append(e / e.sum(axis=0, keepdims=True))
        idx_rows.append(eidx)
    idx_ref[...] = jnp.concatenate(idx_rows, axis=0).T.astype(jnp.int32)
    gate_ref[...] = jnp.concatenate(gate_rows, axis=0).T


def peer_select(qp, subkeys_bf16, tm=128):
    rows, w = qp.shape
    tm = min(tm, rows)
    blk = lambda i: (i, 0)
    return pl.pallas_call(
        _peer_select_kernel,
        grid=(rows // tm,),
        in_specs=[pl.BlockSpec((tm, w), blk), pl.BlockSpec(subkeys_bf16.shape, lambda i: (0, 0, 0, 0))],
        out_specs=[pl.BlockSpec((tm, PEER_K), blk), pl.BlockSpec((tm, PEER_K), blk)],
        out_shape=[jax.ShapeDtypeStruct((rows, PEER_K), jnp.int32), jax.ShapeDtypeStruct((rows, PEER_K), F32)],
        compiler_params=_params(32, ("arbitrary",)),
        name="peer_select",
    )(qp, subkeys_bf16)


def _pack_kernel(u_ref, v_ref, o_ref):
    half = u_ref.shape[1] // 2
    o_ref[:, 0, :half] = u_ref[:, :half].astype(BF16)
    o_ref[:, 0, half:] = v_ref[:, :half].astype(BF16)
    o_ref[:, 1, :half] = u_ref[:, half:].astype(BF16)
    o_ref[:, 1, half:] = v_ref[:, half:].astype(BF16)


def _select_pack_kernel(q_ref, sk_ref, u_ref, v_ref, idx_ref, gate_ref, tab_ref):
    _peer_select_kernel(q_ref, sk_ref, idx_ref, gate_ref)
    _pack_kernel(u_ref, v_ref, tab_ref)


def peer_select_pack(qp, subkeys_bf16, peer_u, peer_v, tm=128):
    rows, w = qp.shape
    tm = min(tm, rows)
    steps = rows // tm
    e, d = peer_u.shape
    assert e % steps == 0
    tr = e // steps
    blk = lambda i: (i, 0)
    return pl.pallas_call(
        _select_pack_kernel,
        grid=(steps,),
        in_specs=[pl.BlockSpec((tm, w), blk), pl.BlockSpec(subkeys_bf16.shape, lambda i: (0, 0, 0, 0)),
                  pl.BlockSpec((tr, d), blk), pl.BlockSpec((tr, d), blk)],
        out_specs=[pl.BlockSpec((tm, PEER_K), blk), pl.BlockSpec((tm, PEER_K), blk),
                   pl.BlockSpec((tr, 2, d), lambda i: (i, 0, 0))],
        out_shape=[jax.ShapeDtypeStruct((rows, PEER_K), jnp.int32), jax.ShapeDtypeStruct((rows, PEER_K), F32),
                   jax.ShapeDtypeStruct((e, 2, d), BF16)],
        compiler_params=_params(48, ("arbitrary",)),
        name="peer_select_pack",
    )(qp, subkeys_bf16, peer_u, peer_v)


def _peer_mix_kernel(idx_ref, nidx_ref, gate_ref, tab_ref, x1_ref, sc_ref, sh_ref, g2_ref, lg_ref, lb_ref, o_ref,
                     buf_a, buf_b, sem_a, sem_b, x_ref, y_ref, *, tt, nsteps, alpha):
    i = pl.program_id(0)
    half = x_ref.shape[1] // 2
    nchunk = half // LANES
    x_ref[...] = (x1_ref[...] * (1.0 + sc_ref[...]) + sh_ref[...]).reshape(x_ref.shape)

    def issue(ids, row, buf, sem, t):
        for k in range(PEER_K):
            pltpu.make_async_copy(tab_ref.at[ids[row, k]], buf.at[t, pl.ds(2 * k, 2), :],
                                  sem.at[t]).start(priority=k % 2)

    def wait(buf, sem, t):
        pltpu.make_async_copy(buf.at[t], buf.at[t], sem.at[t]).wait()

    @pl.when(i == 0)
    def _():
        def first(t, c):
            issue(idx_ref, t, buf_a, sem_a, t)
            return c
        lax.fori_loop(0, tt, first, 0)

    kr = 2 * PEER_K
    sub = lax.broadcasted_iota(jnp.int32, (kr, LANES), 0)
    even = sub % 2 == 0
    pair_eye = sub // 2 == lax.broadcasted_iota(jnp.int32, (kr, LANES), 1)
    row_id = lax.broadcasted_iota(jnp.int32, (tt, LANES), 0)
    y_ref[...] = jnp.zeros(y_ref.shape, F32)

    def mix(buf, t, row0):
        x = x_ref[pl.ds(row0 + t, 1), :]
        acc = jnp.zeros((kr // 8, 8, LANES), F32)
        for ch in range(nchunk):
            cs = slice(ch * LANES, (ch + 1) * LANES)
            xs = jnp.where(even[:8], x[:, cs], x[:, half + ch * LANES:half + (ch + 1) * LANES])
            acc = acc + buf[t, :, cs].astype(F32).reshape(kr // 8, 8, LANES) * xs[None]
        acc = acc.reshape(kr, LANES)
        acc = acc + jnp.where(even, pltpu.roll(acc, kr - 1, 0), pltpu.roll(acc, 1, 0))
        a = acc.sum(axis=1, keepdims=True)
        gate = jnp.sum(jnp.where(pair_eye, gate_ref[pl.ds(row0 + t, 1), :], 0.0), axis=1, keepdims=True)
        act = 0.5 * a * (1.0 + lax.erf(a * (2.0 ** -0.5))) * gate
        act_b = jnp.broadcast_to(act, (kr, LANES))
        mine = row_id == t
        rows = slice(row0, row0 + tt)
        for ch in range(nchunk):
            cs = slice(half + ch * LANES, half + (ch + 1) * LANES)
            s = (act_b * buf[t, :, cs].astype(F32)).reshape(kr // 8, 8, LANES).sum(axis=0)
            s = s + pltpu.roll(s, 2, 0)
            s = s + pltpu.roll(s, 4, 0)
            for r, c0 in ((0, ch * LANES), (1, half + ch * LANES)):
                ys = slice(c0, c0 + LANES)
                y_ref[rows, ys] = jnp.where(mine, s[r:r + 1, :], y_ref[rows, ys])

    def token_a(t, c):
        wait(buf_a, sem_a, t)
        issue(idx_ref, tt + t, buf_b, sem_b, t)
        mix(buf_a, t, 0)
        return c

    def token_b(t, c):
        wait(buf_b, sem_b, t)
        issue(nidx_ref, t, buf_a, sem_a, t)
        mix(buf_b, t, tt)
        return c

    lax.fori_loop(0, tt, token_a, 0)
    lax.fori_loop(0, tt, token_b, 0)

    @pl.when(i == nsteps - 1)
    def _():
        for t in range(tt):
            wait(buf_a, sem_a, t)

    z = alpha * x1_ref[...] + g2_ref[...] * y_ref[...].reshape(x1_ref.shape)
    o_ref[...] = _layer_norm(z, lg_ref[...], lb_ref[...])


def peer_mix_ln2(eidx, gate, table, x1_3, sc, sh, g2, ln_g, ln_b, alpha, tt=8):
    g, t, d = x1_3.shape
    rows = g * t
    tb = 2 * tt
    nsteps = rows // tb
    bt = min(tb, t)
    bg = tb // bt
    nt = t // bt
    blk = lambda i: (i, 0)
    nxt = lambda i: ((i + 1) % nsteps, 0)
    tok3 = pl.BlockSpec((bg, bt, d), lambda i: (i // nt, i % nt, 0))
    grp = pl.BlockSpec((bg, 1, d), lambda i: (i // nt, 0, 0))
    vec = pl.BlockSpec((1, d), lambda i: (0, 0))
    return pl.pallas_call(
        functools.partial(_peer_mix_kernel, tt=tt, nsteps=nsteps, alpha=alpha),
        grid=(nsteps,),
        in_specs=[pl.BlockSpec((tb, PEER_K), blk, memory_space=pltpu.SMEM),
                  pl.BlockSpec((tb, PEER_K), nxt, memory_space=pltpu.SMEM),
                  pl.BlockSpec((tb, PEER_K), blk),
                  pl.BlockSpec(memory_space=pl.ANY),
                  tok3, grp, grp, grp, vec, vec],
        out_specs=tok3,
        out_shape=jax.ShapeDtypeStruct((g, t, d), F32),
        scratch_shapes=[pltpu.VMEM((tt, 2 * PEER_K, d), BF16), pltpu.VMEM((tt, 2 * PEER_K, d), BF16),
                        pltpu.SemaphoreType.DMA((tt,)), pltpu.SemaphoreType.DMA((tt,)),
                        pltpu.VMEM((tb, d), F32), pltpu.VMEM((tb, d), F32)],
        compiler_params=_params(56, ("arbitrary",)),
        name="peer_mix",
    )(eidx, eidx, gate, table, x1_3, sc, sh, g2, ln_g, ln_b)


def _layer(x3, mods, lw, pos0, tg, tt, prev=None):
    g, t, d = x3.shape
    sh1, sc1, g1, sh2, sc2, g2 = mods
    aw = N_HEADS * HEAD_DIM
    alpha = lw["alpha"]

    h = modulate(x3, sc1, sh1, tg, tt)
    q = matmul([(h, lw["w_in"], 0, 0)], aw, name="in_q")
    kv = matmul([(h, lw["w_in"], 0, aw)], 2 * KV_WIDTH, name="in_kv")
    u = matmul([(h, lw["w_in"], 0, aw + 2 * KV_WIDTH)], lw["w_in"].shape[1] - aw - 2 * KV_WIDTH, name="in_u")

    tabs = rope_tables(pos0 + jnp.arange(t, dtype=jnp.int32))
    if prev is None:
        attn, krot = attn_prompt(q, kv, tabs, lw["sinks"], g, t)
        state16 = jnp.zeros((g, POOL_HALO, u.shape[1]), F32)
        new_k = krot.reshape(g, t, KV_WIDTH)[:, -WINDOW:].reshape(g, WINDOW, N_KV_HEADS, HEAD_DIM)
        new_v = kv.reshape(g, t, 2 * KV_WIDTH)[:, -WINDOW:, KV_WIDTH:].reshape(g, WINDOW, N_KV_HEADS, HEAD_DIM)
    else:
        cache_k, cache_v, state = prev
        wc = cache_k.shape[1]
        attn, krot = attn_sample(q, kv, cache_k.reshape(g, wc, KV_WIDTH), cache_v.reshape(g, wc, KV_WIDTH),
                                 tabs, lw["sinks"], t)
        state16 = jnp.pad(state, ((0, 0), (POOL_HALO - POOL_STATE, 0), (0, 0)))
        new_k = jnp.concatenate([cache_k, krot.reshape(g, t, N_KV_HEADS, HEAD_DIM)], axis=1)[:, -wc:]
        new_v = jnp.concatenate([cache_v, kv[:, KV_WIDTH:].reshape(g, t, N_KV_HEADS, HEAD_DIM)], axis=1)[:, -wc:]
    u3 = u.reshape(g, t, u.shape[1])
    new_pool = jnp.concatenate([state16, u3], axis=1)[:, -POOL_STATE:]
    pooled = pool_mix(u3, state16, lw["pool_w"], lw["pool_scale"], min(tg, POOL_TG), tt, pos0)

    y1 = matmul([(attn, lw["w_out"], 0, 0), (pooled, lw["w_out"], 1, 0)], d, name="out_proj")
    ln_tt = min(t, LN_ROWS)
    ln_tg = LN_ROWS // ln_tt
    x1, h2b = ln1_modulate(x3, y1, g1, sc2, sh2, lw["ln1_g"], lw["ln1_b"], ln_tg, ln_tt, alpha)

    qp = matmul([(h2b, lw["peer_wq"], 0, 0)], lw["peer_wq"].shape[1], name="peer_query")
    if "table" not in lw:
        eidx, gate, lw["table"] = peer_select_pack(qp, lw["subkeys"], lw["peer_u"], lw["peer_v"])
    else:
        eidx, gate = peer_select(qp, lw["subkeys"])
    out = peer_mix_ln2(eidx, gate, lw["table"], x1.reshape(g, t, d), sc2, sh2, g2, lw["ln2_g"], lw["ln2_b"], alpha)
    return out, (new_k, new_v, new_pool)


def kernel(x_prompt, x_sample, cache_k, cache_v, state_pool, c_prompt, c_sample, w_ada, b_ada, w_in, sinks,
           pool_w, pool_scale, w_out, ln1_g, ln1_b, peer_wq, peer_subkeys, peer_u, peer_v, ln2_g, ln2_b):
    depth = w_ada.shape[0]
    bp, seq, d = x_prompt.shape
    bs, t_new, _ = x_sample.shape
    past_len = PAST_LEN
    alpha = (2 * depth) ** 0.25
    aw = N_HEADS * HEAD_DIM
    n_c = bp + bs
    pad = (-n_c) % 8

    xp, xs = x_prompt, x_sample
    outs = [[] for _ in range(6)]
    for l in range(depth):
        c_all = jnp.concatenate([c_prompt, c_sample, jnp.zeros((pad, d), F32)], axis=0)
        mod = adaln(c_all, w_ada[l], b_ada[l][None, :])
        mods_p = tuple(mod[i, :bp].reshape(bp, 1, d) for i in range(N_MOD))
        mods_s = tuple(mod[i, bp:n_c].reshape(bs, 1, d) for i in range(N_MOD))
        lw = dict(
            alpha=alpha, w_in=w_in[l], w_out=w_out[l], peer_wq=peer_wq[l],
            sinks=sinks[l], pool_w=pool_w[l].astype(BF16), pool_scale=pool_scale[l][None, :],
            ln1_g=ln1_g[l][None, :], ln1_b=ln1_b[l][None, :], ln2_g=ln2_g[l][None, :], ln2_b=ln2_b[l][None, :],
            subkeys=peer_subkeys[l].astype(BF16), peer_u=peer_u[l], peer_v=peer_v[l],
        )
        xp, (k1, v1, p1) = _layer(xp, mods_p, lw, 0, 1, 512)
        xs, (k2, v2, p2) = _layer(xs, mods_s, lw, past_len, 64, t_new, prev=(cache_k[l], cache_v[l], state_pool[l]))
        for lst, val in zip(outs, (k1, v1, p1, k2, v2, p2)):
            lst.append(val)
    return (xp, xs) + tuple(jnp.stack(o) for o in outs)
```
